```python
import math, functools
import jax, jax.numpy as jnp
from jax import lax
import numpy as np

D_MODEL = 1024
BATCH = 8
SEQ = 4096
DEPTH = 2
DEC_BATCH = 128
DEC_SEQ = 8
PAST_LEN = 16384
PAGE_SIZE = 128

CONV_DIM = D_MODEL // 2
CONV_WIDTH = 3
MLA_HEADS = 8
MLA_NOPE = 64
MLA_ROPE = 32
MLA_QK = MLA_NOPE + MLA_ROPE
MLA_V = 64
Q_LORA = 3 * D_MODEL // 8
KV_LORA = D_MODEL // 4
ATTN_BLOCK = 128
RET_HEADS = 4
RET_DK = 64
RET_DV = 128
RET_CHUNK = 128
D_FF = 4 * D_MODEL
N_BRANCH = 3
ROPE_THETA = 10000.0
EPS = 1e-6
NEG = -1e30

SEG_SIZES = (CONV_DIM, CONV_DIM, CONV_DIM,
             Q_LORA, KV_LORA, MLA_ROPE,
             RET_HEADS * RET_DK, RET_HEADS * RET_DK, RET_HEADS * RET_DV, RET_HEADS * RET_DV,
             N_BRANCH * D_MODEL)
D_IN = sum(SEG_SIZES)

kernel_name = "hybrid_conv_mla_retention_decode_step"


def rmsnorm(x, g):
    xf = x.astype(jnp.float32)
    y = xf * lax.rsqrt(jnp.mean(xf * xf, axis=-1, keepdims=True) + EPS)
    return (y * g.astype(jnp.float32)).astype(x.dtype)


def head_rms(x):
    xf = x.astype(jnp.float32)
    return (xf * lax.rsqrt(jnp.mean(xf * xf, axis=-1, keepdims=True) + EPS)).astype(x.dtype)


def rope(x, pos):
    half = x.shape[-1] // 2
    inv = ROPE_THETA ** (-jnp.arange(half, dtype=jnp.float32) / half)
    ang = pos.astype(jnp.float32)[:, None] * inv[None, :]
    cos = jnp.cos(ang)[:, None, :]
    sin = jnp.sin(ang)[:, None, :]
    xf = x.astype(jnp.float32)
    x1, x2 = xf[..., :half], xf[..., half:]
    return jnp.concatenate([x1 * cos - x2 * sin, x2 * cos + x1 * sin], axis=-1).astype(x.dtype)


def split_cols(z):
    cuts = np.cumsum(SEG_SIZES)[:-1].tolist()
    return jnp.split(z, cuts, axis=-1)


def short_conv(b_gate, c_gate, h, buf, w_conv):
    u = c_gate * h
    ext = jnp.concatenate([buf, u], axis=1)
    L = u.shape[1]
    v = sum(ext[:, j:j + L] * w_conv[j] for j in range(CONV_WIDTH))
    return b_gate * v, ext[:, -(CONV_WIDTH - 1):]


def mla_expand(ckv_n, kpe_r, w_ukv, g_kn):
    kv = (ckv_n @ w_ukv).reshape(*ckv_n.shape[:-1], MLA_HEADS, MLA_NOPE + MLA_V)
    k_nope, v = kv[..., :MLA_NOPE], kv[..., MLA_NOPE:]
    k_pe = jnp.broadcast_to(kpe_r[..., None, :], (*kpe_r.shape[:-1], MLA_HEADS, MLA_ROPE))
    k = rmsnorm(jnp.concatenate([k_nope, k_pe], axis=-1), g_kn)
    return k, v


def attend(q, k, v, q_pos, k_pos):
    s = jnp.einsum('bqhd,bkhd->bhqk', q, k).astype(jnp.float32) * (MLA_QK ** -0.5)
    mask = k_pos[None, :] <= q_pos[:, None]
    p = jax.nn.softmax(jnp.where(mask, s, NEG), axis=-1).astype(v.dtype)
    return jnp.einsum('bhqk,bkhe->bqhe', p, v)


def prompt_attention(q, ckv_n, kpe_r, w_ukv, g_kn):
    B, S = q.shape[:2]
    k, v = mla_expand(ckv_n, kpe_r, w_ukv, g_kn)
    pos = jnp.arange(S)
    blk = math.gcd(S, ATTN_BLOCK)
    nb = S // blk
    qb = q.reshape(B, nb, blk, MLA_HEADS, MLA_QK).swapaxes(0, 1)
    pb = pos.reshape(nb, blk)
    out = lax.map(lambda a: attend(a[0], k, v, a[1], pos), (qb, pb))
    return out.swapaxes(0, 1).reshape(B, S, MLA_HEADS, MLA_V)


def sample_attention(q, ckv_n, kpe_r, w_ukv, g_kn, cache_ckv_l, cache_kpe_l, page_table):
    L = q.shape[1]
    past = page_table.shape[1] * PAGE_SIZE
    k_pos = jnp.arange(past + L)
    q_pos = past + jnp.arange(L)

    def one(a):
        pt, q1, c1, p1 = a
        c_all = jnp.concatenate([cache_ckv_l[pt].reshape(past, KV_LORA), c1], axis=0)
        p_all = jnp.concatenate([cache_kpe_l[pt].reshape(past, MLA_ROPE), p1], axis=0)
        k, v = mla_expand(c_all[None], p_all[None], w_ukv, g_kn)
        return attend(q1[None], k, v, q_pos, k_pos)[0]

    return lax.map(one, (page_table, q, ckv_n, kpe_r))


def retention(q, k, v, s0):
    B, L, H, _ = q.shape
    chunk = math.gcd(L, RET_CHUNK)
    n = L // chunk
    log_g = jnp.log1p(-(2.0 ** (-5.0 - jnp.arange(H, dtype=jnp.float32))))
    idx = jnp.arange(chunk, dtype=jnp.float32)
    diff = idx[:, None] - idx[None, :]
    dmat = jnp.where(diff >= 0, jnp.exp(log_g[:, None, None] * jnp.maximum(diff, 0.0)), 0.0)
    q_decay = jnp.exp(log_g[None, :] * (idx[:, None] + 1.0))[None, :, :, None]
    k_decay = jnp.exp(log_g[None, :] * (chunk - 1.0 - idx[:, None]))[None, :, :, None]
    c_decay = jnp.exp(log_g * chunk)[None, :, None, None]

    def blocks(t):
        return t.astype(jnp.float32).reshape(B, n, chunk, H, t.shape[-1]).swapaxes(0, 1)

    def step(S, inp):
        qb, kb, vb = inp
        sc = jnp.einsum('bihd,bjhd->bhij', qb, kb) * dmat
        o = jnp.einsum('bhij,bjhe->bihe', sc, vb) + jnp.einsum('bihd,bhde->bihe', qb, S) * q_decay
        S = S * c_decay + jnp.einsum('bjhd,bjhe->bhde', kb * k_decay, vb)
        return S, o

    S, o = lax.scan(step, s0.astype(jnp.float32), (blocks(q), blocks(k), blocks(v)))
    o = o.swapaxes(0, 1).reshape(B, L, H, v.shape[-1])
    return o.astype(v.dtype), S.astype(s0.dtype)


def layer(x, pos, conv_buf, ret_state, attn_fn, g_mix, w_in, w_conv, w_conv_out, g_q_lat, w_uq,
          g_kv_lat, w_ukv, g_qn, g_kn, w_mla_out, w_ret_out, w_o, g_ffn, w_up, w_down):
    B, L, _ = x.shape
    h = rmsnorm(x, g_mix)
    cb, cc, ch, cq, ckv, kpe, rq, rk, rv, rg, gl = split_cols(h @ w_in)

    y_conv, new_buf = short_conv(cb, cc, ch, conv_buf, w_conv)
    br_a = y_conv @ w_conv_out

    q = (rmsnorm(cq, g_q_lat) @ w_uq).reshape(B, L, MLA_HEADS, MLA_QK)
    q = rmsnorm(jnp.concatenate([q[..., :MLA_NOPE], rope(q[..., MLA_NOPE:], pos)], axis=-1), g_qn)
    ckv_n = rmsnorm(ckv, g_kv_lat)
    kpe_r = rope(kpe[:, :, None, :], pos)[:, :, 0, :]
    o_mla = attn_fn(q, ckv_n, kpe_r, w_ukv, g_kn)
    br_b = o_mla.reshape(B, L, MLA_HEADS * MLA_V) @ w_mla_out

    rq = rope(rq.reshape(B, L, RET_HEADS, RET_DK), pos)
    rk = rope(rk.reshape(B, L, RET_HEADS, RET_DK), pos) * (RET_DK ** -0.5)
    o_r, new_state = retention(rq, rk, rv.reshape(B, L, RET_HEADS, RET_DV), ret_state)
    br_c = (jax.nn.silu(rg) * head_rms(o_r).reshape(B, L, RET_HEADS * RET_DV)) @ w_ret_out

    g = jax.nn.sigmoid(gl.astype(jnp.float32)).astype(x.dtype).reshape(B, L, N_BRANCH, D_MODEL)
    mixed = g[:, :, 0] * br_a + g[:, :, 1] * br_b + g[:, :, 2] * br_c
    x = x + mixed @ w_o

    x = x + jnp.square(jax.nn.relu(rmsnorm(x, g_ffn) @ w_up)) @ w_down
    return x, ckv_n, kpe_r, new_buf, new_state


def setup_inputs(seed: int = 0) -> dict:
    key = jax.random.key(seed)
    ks = jax.random.split(key, 32)
    f32 = jnp.float32
    n_pages = PAST_LEN // PAGE_SIZE
    n_used = DEC_BATCH * n_pages
    n_phys = n_used + max(1, n_used // 4)

    def w(k, shape, fan_in):
        return jax.random.normal(k, shape, f32) * (fan_in ** -0.5)

    def gain(k, shape):
        return 1.0 + 0.01 * jax.random.normal(k, shape, f32)

    page_table = jax.random.permutation(ks[6], n_phys)[:n_used].reshape(DEC_BATCH, n_pages).astype(jnp.int32)
    return {
        "x_prompt": jax.random.normal(ks[0], (BATCH, SEQ, D_MODEL), f32),
        "x_sample": jax.random.normal(ks[1], (DEC_BATCH, DEC_SEQ, D_MODEL), f32),
        "cache_ckv": jax.random.normal(ks[2], (DEPTH, n_phys, PAGE_SIZE, KV_LORA), f32),
        "cache_kpe": jax.random.normal(ks[3], (DEPTH, n_phys, PAGE_SIZE, MLA_ROPE), f32),
        "state_conv": jax.random.normal(ks[4], (DEPTH, DEC_BATCH, CONV_WIDTH - 1, CONV_DIM), f32),
        "state_ret": jax.random.normal(ks[5], (DEPTH, DEC_BATCH, RET_HEADS, RET_DK, RET_DV), f32),
        "page_table": page_table,
        "g_mix": gain(ks[7], (DEPTH, D_MODEL)),
        "w_in": w(ks[8], (DEPTH, D_MODEL, D_IN), D_MODEL),
        "w_conv": w(ks[9], (DEPTH, CONV_WIDTH, CONV_DIM), CONV_WIDTH),
        "w_conv_out": w(ks[10], (DEPTH, CONV_DIM, D_MODEL), CONV_DIM),
        "g_q_lat": gain(ks[11], (DEPTH, Q_LORA)),
        "w_uq": w(ks[12], (DEPTH, Q_LORA, MLA_HEADS * MLA_QK), Q_LORA),
        "g_kv_lat": gain(ks[13], (DEPTH, KV_LORA)),
        "w_ukv": w(ks[14], (DEPTH, KV_LORA, MLA_HEADS * (MLA_NOPE + MLA_V)), KV_LORA),
        "g_qn": gain(ks[15], (DEPTH, MLA_QK)),
        "g_kn": gain(ks[16], (DEPTH, MLA_QK)),
        "w_mla_out": w(ks[17], (DEPTH, MLA_HEADS * MLA_V, D_MODEL), MLA_HEADS * MLA_V),
        "w_ret_out": w(ks[18], (DEPTH, RET_HEADS * RET_DV, D_MODEL), RET_HEADS * RET_DV),
        "w_o": w(ks[19], (DEPTH, D_MODEL, D_MODEL), D_MODEL),
        "g_ffn": gain(ks[20], (DEPTH, D_MODEL)),
        "w_up": w(ks[21], (DEPTH, D_MODEL, D_FF), D_MODEL),
        "w_down": w(ks[22], (DEPTH, D_FF, D_MODEL), D_FF),
    }


def reference(x_prompt, x_sample, cache_ckv, cache_kpe, state_conv, state_ret, page_table,
              g_mix, w_in, w_conv, w_conv_out, g_q_lat, w_uq, g_kv_lat, w_ukv, g_qn, g_kn,
              w_mla_out, w_ret_out, w_o, g_ffn, w_up, w_down):
    B, S, _ = x_prompt.shape
    L = x_sample.shape[1]
    past = page_table.shape[1] * PAGE_SIZE
    pos_p = jnp.arange(S)
    pos_s = past + jnp.arange(L)
    conv0 = jnp.zeros((B, CONV_WIDTH - 1, CONV_DIM), x_prompt.dtype)
    ret0 = jnp.zeros((B, RET_HEADS, RET_DK, RET_DV), x_prompt.dtype)

    yp, ys = x_prompt, x_sample
    ckv_p, kpe_p, conv_p, ret_p = [], [], [], []
    ckv_s, kpe_s, conv_s, ret_s = [], [], [], []
    for l in range(DEPTH):
        wl = (g_mix[l], w_in[l], w_conv[l], w_conv_out[l], g_q_lat[l], w_uq[l], g_kv_lat[l], w_ukv[l],
              g_qn[l], g_kn[l], w_mla_out[l], w_ret_out[l], w_o[l], g_ffn[l], w_up[l], w_down[l])
        yp, c1, p1, b1, r1 = layer(yp, pos_p, conv0, ret0, prompt_attention, *wl)
        samp_attn = functools.partial(sample_attention, cache_ckv_l=cache_ckv[l],
                                      cache_kpe_l=cache_kpe[l], page_table=page_table)
        ys, c2, p2, b2, r2 = layer(ys, pos_s, state_conv[l], state_ret[l], samp_attn, *wl)
        ckv_p.append(c1); kpe_p.append(p1); conv_p.append(b1); ret_p.append(r1)
        ckv_s.append(c2); kpe_s.append(p2); conv_s.append(b2); ret_s.append(r2)

    return (yp, ys,
            jnp.stack(ckv_p), jnp.stack(kpe_p), jnp.stack(conv_p), jnp.stack(ret_p),
            jnp.stack(ckv_s), jnp.stack(kpe_s), jnp.stack(conv_s), jnp.stack(ret_s))
```

```python
import functools
import math

import jax
import jax.numpy as jnp
import numpy as np
from jax import lax
from jax.experimental import pallas as pl
from jax.experimental.pallas import tpu as pltpu

F32 = jnp.float32
BF16 = jnp.bfloat16

D_MODEL = 1024
PAGE = 128
CONV_DIM = 512
CONV_W = 3
HEADS = 8
NOPE = 64
ROPE = 32
QK = NOPE + ROPE
VDIM = 64
Q_LORA = 384
KV_LORA = 256
R_HEADS = 4
R_DK = 64
R_DV = 128
R_CHUNK = 128
D_FF = 4096
THETA = 10000.0
EPS = 1e-6
NEG = -1e30

LANE = 128
SUBLANE = 8
HT = LANE

Z_CONV = 0
Z_RET = 1536
Z_GATE = 3072
Z_MLA = 6144
Z_W = 6912
W_CONVBLK = 3 * CONV_DIM
W_RETBLK = 2 * R_HEADS * R_DK + 2 * R_HEADS * R_DV
W_MLABLK = Q_LORA + KV_LORA + LANE

VMEM_LIMIT = 56 * 1024 * 1024


def _dot(a, b):
    return jnp.dot(a, b, preferred_element_type=F32)


def _dot_nt(a, b):
    return lax.dot_general(a, b, (((1,), (1,)), ((), ())), preferred_element_type=F32)


def _params(sem):
    return pltpu.CompilerParams(dimension_semantics=sem, vmem_limit_bytes=VMEM_LIMIT)


def _rope_rows(x, tab_ref, half):
    return (x * tab_ref[0] + pltpu.roll(x, LANE - half, 1) * tab_ref[1]
            + pltpu.roll(x, half, 1) * tab_ref[2])


def _inproj_kernel(x_ref, g_ref, w_ref, wrk_ref, z_ref, rkt_ref, h_scr):
    @pl.when(pl.program_id(1) == 0)
    def _():
        x = x_ref[...]
        ms = jnp.mean(x * x, axis=-1, keepdims=True)
        hb = (x * lax.rsqrt(ms + EPS) * g_ref[...]).astype(BF16)
        h_scr[...] = hb
        rkt_ref[...] = _dot_nt(wrk_ref[...], hb)

    z_ref[...] = _dot(h_scr[...], w_ref[...])


def _inproj(x, g, w_perm, w_rk_t):
    t = x.shape[0]
    tm = min(1024, t)
    tn = 2304
    return pl.pallas_call(
        _inproj_kernel,
        grid=(t // tm, Z_W // tn),
        in_specs=[
            pl.BlockSpec((tm, D_MODEL), lambda i, j: (i, 0)),
            pl.BlockSpec((1, D_MODEL), lambda i, j: (0, 0)),
            pl.BlockSpec((D_MODEL, tn), lambda i, j: (0, j)),
            pl.BlockSpec((R_HEADS * R_DK, D_MODEL), lambda i, j: (0, 0)),
        ],
        out_specs=[
            pl.BlockSpec((tm, tn), lambda i, j: (i, j)),
            pl.BlockSpec((R_HEADS * R_DK, tm), lambda i, j: (0, i)),
        ],
        out_shape=[jax.ShapeDtypeStruct((t, Z_W), F32),
                   jax.ShapeDtypeStruct((R_HEADS * R_DK, t), F32)],
        scratch_shapes=[pltpu.VMEM((tm, D_MODEL), BF16)],
        compiler_params=_params(("parallel", "arbitrary")),
        name="inproj",
    )(x, g, w_perm, w_rk_t)


def _mlaprep_kernel(z_ref, qtab_ref, ktab_ref, gq_ref, wuq_ref, gkv_ref, gqn_ref,
                    *rest, expand):
    if expand:
        wk_ref, pmat_ref, wv_ref, gkn_ref, q_ref, ckv_ref, kpe_ref, k_ref, v_ref = rest
    else:
        q_ref, ckv_ref, kpe_ref = rest
    cq = z_ref[:, :Q_LORA]
    ckv = z_ref[:, Q_LORA:Q_LORA + KV_LORA]
    kpe = z_ref[:, Q_LORA + KV_LORA:]

    cqn = cq * lax.rsqrt(jnp.mean(cq * cq, axis=-1, keepdims=True) + EPS) * gq_ref[...]
    qf = _dot(cqn.astype(BF16), wuq_ref[...])
    for h in range(HEADS):
        xr = _rope_rows(qf[:, h * HT:(h + 1) * HT], qtab_ref, ROPE // 2)
        ms = jnp.sum(xr * xr, axis=-1, keepdims=True) * (1.0 / QK)
        q_ref[:, h * HT:(h + 1) * HT] = (xr * lax.rsqrt(ms + EPS) * gqn_ref[...]).astype(BF16)

    ckvn = ckv * lax.rsqrt(jnp.mean(ckv * ckv, axis=-1, keepdims=True) + EPS) * gkv_ref[...]
    ckv_ref[...] = ckvn
    kper = _rope_rows(kpe, ktab_ref, ROPE // 2)
    kpe_ref[...] = kper[:, :ROPE]

    if expand:
        cb = ckvn.astype(BF16)
        p_hi = kper.astype(BF16)
        p_lo = (kper - p_hi.astype(F32)).astype(BF16)
        kf = _dot(cb, wk_ref[...]) + _dot(p_hi, pmat_ref[...]) + _dot(p_lo, pmat_ref[...])
        for h in range(HEADS):
            xk = kf[:, h * HT:(h + 1) * HT]
            ms = jnp.sum(xk * xk, axis=-1, keepdims=True) * (1.0 / QK)
            k_ref[:, h * HT:(h + 1) * HT] = (xk * lax.rsqrt(ms + EPS) * gkn_ref[...]).astype(BF16)
        v_ref[...] = _dot(cb, wv_ref[...]).astype(BF16)


def _mlaprep(z, qtab, ktab, tm, gq, wuq, gkv, gqn, expand_args):
    t = z.shape[0]
    n_seq_tiles = qtab.shape[1] // tm
    expand = expand_args is not None
    const = lambda shape: pl.BlockSpec(shape, lambda i: (0,) * len(shape))
    tab = pl.BlockSpec((3, tm, LANE), lambda i: (0, i % n_seq_tiles, 0))
    in_specs = [
        pl.BlockSpec((tm, W_MLABLK), lambda i: (i, Z_MLA // W_MLABLK)),
        tab, tab,
        const((1, Q_LORA)), const((Q_LORA, HEADS * HT)), const((1, KV_LORA)), const((1, HT)),
    ]
    args = [z, qtab, ktab, gq, wuq, gkv, gqn]
    out_specs = [pl.BlockSpec((tm, HEADS * HT), lambda i: (i, 0)),
                 pl.BlockSpec((tm, KV_LORA), lambda i: (i, 0)),
                 pl.BlockSpec((tm, ROPE), lambda i: (i, 0))]
    out_shape = [jax.ShapeDtypeStruct((t, HEADS * HT), BF16),
                 jax.ShapeDtypeStruct((t, KV_LORA), F32),
                 jax.ShapeDtypeStruct((t, ROPE), F32)]
    if expand:
        wk, pmat, wv, gkn = expand_args
        in_specs += [const((KV_LORA, HEADS * HT)), const((LANE, HEADS * HT)),
                     const((KV_LORA, HEADS * VDIM)), const((1, HT))]
        args += [wk, pmat, wv, gkn]
        out_specs += [pl.BlockSpec((tm, HEADS * HT), lambda i: (i, 0)),
                      pl.BlockSpec((tm, HEADS * VDIM), lambda i: (i, 0))]
        out_shape += [jax.ShapeDtypeStruct((t, HEADS * HT), BF16),
                      jax.ShapeDtypeStruct((t, HEADS * VDIM), BF16)]
    return pl.pallas_call(
        functools.partial(_mlaprep_kernel, expand=expand),
        grid=(t // tm,),
        in_specs=in_specs, out_specs=out_specs, out_shape=out_shape,
        compiler_params=_params(("parallel",)),
        name="mlaprep_expand" if expand else "mlaprep",
    )(*args)


def _flash_kernel(q_ref, k_ref, v_ref, o_ref, *, tq, tk):
    qi = pl.program_id(2)
    lane = lax.broadcasted_iota(jnp.int32, (tq, LANE), 1)
    row = lax.broadcasted_iota(jnp.int32, (tq, tk), 0)
    col = lax.broadcasted_iota(jnp.int32, (tq, tk), 1)
    outs = []
    for h in range(2):
        q = q_ref[:, h * HT:(h + 1) * HT]

        def tile(ki, carry, masked):
            m, l, acc = carry
            start = pl.multiple_of(ki * tk, tk)
            k = k_ref[pl.ds(start, tk), h * HT:(h + 1) * HT]
            v = v_ref[pl.ds(start, tk), :]
            s = _dot_nt(q, k)
            if masked:
                s = jnp.where(col <= row, s, NEG)
            m_new = jnp.maximum(m, jnp.max(s, axis=-1, keepdims=True))
            alpha = jnp.exp(m - m_new)
            p = jnp.exp(s - m_new)
            l = alpha * l + jnp.sum(p, axis=-1, keepdims=True)
            acc = alpha * acc + _dot(p.astype(BF16), v)
            return m_new, l, acc

        init = (jnp.full((tq, 1), NEG, F32), jnp.zeros((tq, 1), F32), jnp.zeros((tq, LANE), F32))
        carry = lax.fori_loop(0, qi, lambda ki, c: tile(ki, c, False), init)
        m, l, acc = tile(qi, carry, True)
        outs.append(acc / l)
    o_ref[...] = jnp.where(lane < VDIM, outs[0], outs[1]).astype(BF16)


def _flash(q, k, v, batch, seq):
    tq = tk = min(256, seq)
    nq = seq // tq
    return pl.pallas_call(
        functools.partial(_flash_kernel, tq=tq, tk=tk),
        grid=(batch, HEADS // 2, nq),
        in_specs=[
            pl.BlockSpec((tq, 2 * HT), lambda b, hp, i: (b * nq + i, hp)),
            pl.BlockSpec((seq, 2 * HT), lambda b, hp, i: (b, hp)),
            pl.BlockSpec((seq, 2 * VDIM), lambda b, hp, i: (b, hp)),
        ],
        out_specs=pl.BlockSpec((tq, 2 * VDIM), lambda b, hp, i: (b * nq + i, hp)),
        out_shape=jax.ShapeDtypeStruct((batch * seq, HEADS * VDIM), BF16),
        compiler_params=_params(("parallel", "parallel", "arbitrary")),
        name="flash",
    )(q, k, v)


def _qabs_kernel(q_ref, gkn_ref, wcomb_ref, qa_ref, qp_ref):
    nb = qa_ref.shape[0]
    for h in range(HEADS):
        qg = (q_ref[:, h * HT:(h + 1) * HT].astype(F32) * gkn_ref[...]).astype(BF16)
        r = _dot(qg, wcomb_ref[h])
        qa_ref[:, h] = r[:, :KV_LORA].reshape(nb, SUBLANE, KV_LORA)
        qp_ref[:, h] = r[:, KV_LORA:].reshape(nb, SUBLANE, LANE)


def _qabs(q, gkn, wcomb, nb, dec_seq):
    t = q.shape[0]
    return pl.pallas_call(
        _qabs_kernel,
        grid=(1,),
        in_specs=[pl.BlockSpec((t, HEADS * HT), lambda i: (0, 0)),
                  pl.BlockSpec((1, HT), lambda i: (0, 0)),
                  pl.BlockSpec((HEADS, HT, KV_LORA + LANE), lambda i: (0, 0, 0))],
        out_specs=[pl.BlockSpec((nb, HEADS, dec_seq, KV_LORA), lambda i: (0, 0, 0, 0)),
                   pl.BlockSpec((nb, HEADS, dec_seq, LANE), lambda i: (0, 0, 0, 0))],
        out_shape=[jax.ShapeDtypeStruct((nb, HEADS, dec_seq, KV_LORA), F32),
                   jax.ShapeDtypeStruct((nb, HEADS, dec_seq, LANE), F32)],
        compiler_params=_params(("arbitrary",)),
        name="qabs",
    )(q, gkn, wcomb)


def _decode_kernel(pt_ref, qa_ref, qp_ref, wukt_ref, wv_ref, cnew_ref, pnew_ref, *rest,
                   npg, sub):
    del pt_ref
    c_pages = rest[:npg]
    p_pages = rest[npg:2 * npg]
    o_ref = rest[2 * npg]
    cbf, pbf, p2h, p2l, m_scr, l_scr, acc_scr = rest[2 * npg + 1:]
    g = pl.program_id(1)
    ng = pl.num_programs(1)
    rows = HEADS * SUBLANE

    @pl.when(g == 0)
    def _():
        m_scr[...] = jnp.full(m_scr.shape, NEG, F32)
        l_scr[...] = jnp.zeros(l_scr.shape, F32)
        acc_scr[...] = jnp.zeros(acc_scr.shape, F32)
        pbf[...] = jnp.zeros(pbf.shape, BF16)
        p2h[...] = jnp.zeros(p2h.shape, BF16)
        p2l[...] = jnp.zeros(p2l.shape, BF16)

    def stage(i, c, kp):
        cbf[i * PAGE:(i + 1) * PAGE, :] = c.astype(BF16)
        pbf[i * PAGE:(i + 1) * PAGE, :ROPE] = kp.astype(BF16)
        kp2 = kp * kp
        hi = kp2.astype(BF16)
        p2h[i * PAGE:(i + 1) * PAGE, :ROPE] = hi
        p2l[i * PAGE:(i + 1) * PAGE, :ROPE] = (kp2 - hi.astype(F32)).astype(BF16)

    qa = qa_ref[0].astype(BF16)
    qp = qp_ref[0].astype(BF16)
    ones = jnp.ones((SUBLANE, LANE), BF16)

    def attend(start, width, mask):
        ct = cbf[start:start + width, :]
        pt = pbf[start:start + width, :]
        knt = _dot_nt(wukt_ref[...], ct)
        kpsq = (_dot_nt(ones, p2h[start:start + width, :])
                + _dot_nt(ones, p2l[start:start + width, :]))[0:1]
        s = _dot_nt(qa, ct) + _dot_nt(qp, pt)
        parts = []
        for h in range(HEADS):
            blk = knt[h * NOPE:(h + 1) * NOPE]
            nsq = jnp.sum(blk * blk, axis=0, keepdims=True)
            rs = lax.rsqrt((nsq + kpsq) * (1.0 / QK) + EPS)
            parts.append(s[h * SUBLANE:(h + 1) * SUBLANE] * rs)
        s = jnp.concatenate(parts, axis=0)
        if mask is not None:
            s = jnp.where(mask, s, NEG)
        m = m_scr[...]
        m_new = jnp.maximum(m, jnp.max(s, axis=-1, keepdims=True))
        alpha = jnp.exp(m - m_new)
        p = jnp.exp(s - m_new)
        l_scr[...] = alpha * l_scr[...] + jnp.sum(p, axis=-1, keepdims=True)
        acc_scr[...] = alpha * acc_scr[...] + _dot(p.astype(BF16), ct)
        m_scr[...] = m_new

    for i in range(npg):
        stage(i, c_pages[i][...], p_pages[i][...])
    for j in range(npg * PAGE // sub):
        attend(j * sub, sub, None)

    @pl.when(g == ng - 1)
    def _():
        stage(0, cnew_ref[0], pnew_ref[0])
        r = lax.broadcasted_iota(jnp.int32, (rows, PAGE), 0)
        t = lax.broadcasted_iota(jnp.int32, (rows, PAGE), 1)
        attend(0, PAGE, t <= (r % SUBLANE))
        olat = (acc_scr[...] / l_scr[...]).astype(BF16)
        full = _dot(olat, wv_ref[...])
        colh = lax.broadcasted_iota(jnp.int32, (SUBLANE, HEADS * VDIM), 1) // VDIM
        out = jnp.zeros((SUBLANE, HEADS * VDIM), F32)
        for h in range(HEADS):
            out = out + jnp.where(colh == h, full[h * SUBLANE:(h + 1) * SUBLANE], 0.0)
        o_ref[0] = out.astype(BF16)


def _decode(page_table, layer, cache_ckv, cache_kpe, qa, qp, wukt, wv, cnew, pnew):
    nb, n_pages = page_table.shape
    npg = min(16, n_pages)
    sub = min(512, npg * PAGE)
    ngrp = n_pages // npg
    rows = HEADS * SUBLANE

    def page_spec(width, i):
        return pl.BlockSpec((None, None, PAGE, width),
                            lambda b, g, pt: (layer, pt[b, g * npg + i], 0, 0))

    in_specs = [
        pl.BlockSpec((1, rows, KV_LORA), lambda b, g, pt: (b, 0, 0)),
        pl.BlockSpec((1, rows, LANE), lambda b, g, pt: (b, 0, 0)),
        pl.BlockSpec((HEADS * NOPE, KV_LORA), lambda b, g, pt: (0, 0)),
        pl.BlockSpec((KV_LORA, HEADS * VDIM), lambda b, g, pt: (0, 0)),
        pl.BlockSpec((1, PAGE, KV_LORA), lambda b, g, pt: (b, 0, 0)),
        pl.BlockSpec((1, PAGE, ROPE), lambda b, g, pt: (b, 0, 0)),
    ]
    in_specs += [page_spec(KV_LORA, i) for i in range(npg)]
    in_specs += [page_spec(ROPE, i) for i in range(npg)]
    grid_spec = pltpu.PrefetchScalarGridSpec(
        num_scalar_prefetch=1,
        grid=(nb, ngrp),
        in_specs=in_specs,
        out_specs=pl.BlockSpec((1, SUBLANE, HEADS * VDIM), lambda b, g, pt: (b, 0, 0)),
        scratch_shapes=[
            pltpu.VMEM((npg * PAGE, KV_LORA), BF16),
            pltpu.VMEM((npg * PAGE, LANE), BF16),
            pltpu.VMEM((npg * PAGE, LANE), BF16),
            pltpu.VMEM((npg * PAGE, LANE), BF16),
            pltpu.VMEM((rows, 1), F32),
            pltpu.VMEM((rows, 1), F32),
            pltpu.VMEM((rows, KV_LORA), F32),
        ],
    )
    return pl.pallas_call(
        functools.partial(_decode_kernel, npg=npg, sub=sub),
        grid_spec=grid_spec,
        out_shape=jax.ShapeDtypeStruct((nb, SUBLANE, HEADS * VDIM), BF16),
        compiler_params=_params(("parallel", "arbitrary")),
        name="decode",
    )(page_table, qa, qp, wukt, wv, cnew, pnew,
      *([cache_ckv] * npg), *([cache_kpe] * npg))


def _ret_prepare(zr_ref, rkt_ref, rtab_ref, ctab_ref):
    dk = R_HEADS * R_DK
    q = jnp.concatenate(
        [_rope_rows(zr_ref[:, i * LANE:(i + 1) * LANE], rtab_ref, R_DK // 2) for i in range(dk // LANE)],
        axis=1)
    cos_t = ctab_ref[0]
    sin_t = ctab_ref[1]
    half = R_DK // 2
    kparts = []
    for h in range(R_HEADS):
        a = rkt_ref[h * R_DK:h * R_DK + half, :]
        b = rkt_ref[h * R_DK + half:(h + 1) * R_DK, :]
        kparts += [a * cos_t - b * sin_t, b * cos_t + a * sin_t]
    kt = jnp.concatenate(kparts, axis=0) * (R_DK ** -0.5)
    v = zr_ref[:, 2 * dk:2 * dk + R_HEADS * R_DV]
    rg = zr_ref[:, 2 * dk + R_HEADS * R_DV:]
    return q, kt, v, rg


def _ret_finish(o_h, rg_h):
    ms = jnp.mean(o_h * o_h, axis=-1, keepdims=True)
    return (jax.nn.silu(rg_h) * (o_h * lax.rsqrt(ms + EPS))).astype(BF16)


def _head_lane_mask(h, n):
    lane = lax.broadcasted_iota(jnp.int32, (n, R_HEADS * R_DK), 1)
    return (lane // R_DK) == h


def _ret_prompt_kernel(zr_ref, rkt_ref, rtab_ref, ctab_ref, dmat_ref, qdec_ref, kdec_ref, cdec_ref,
                       s0_ref, o_ref, s_ref, s_scr):
    c = pl.program_id(1)

    @pl.when(c == 0)
    def _():
        s_scr[...] = s0_ref[0]

    q, kt, v, rg = _ret_prepare(zr_ref, rkt_ref, rtab_ref, ctab_ref)
    n = q.shape[0]
    ktb = kt.astype(BF16)
    kdb = (kt * kdec_ref[...]).astype(BF16)
    vb = v.astype(BF16)
    s_old = s_scr[...]
    sb = s_old.astype(BF16)
    for h in range(R_HEADS):
        qm = jnp.where(_head_lane_mask(h, n), q, 0.0).astype(BF16)
        vh = vb[:, h * R_DV:(h + 1) * R_DV]
        sc = _dot(qm, ktb) * dmat_ref[h]
        o_h = _dot(sc.astype(BF16), vh) + _dot(qm, sb) * qdec_ref[:, h * R_DV:(h + 1) * R_DV]
        o_ref[:, h * R_DV:(h + 1) * R_DV] = _ret_finish(o_h, rg[:, h * R_DV:(h + 1) * R_DV])
        rows = slice(h * R_DK, (h + 1) * R_DK)
        s_scr[rows, :] = s_old[rows] * cdec_ref[rows, :] + _dot(kdb[rows], vh)
    s_ref[0] = s_scr[...]


def _ret_prompt(z, rkt, rtab, ctab, consts, s0, batch, seq):
    chunk = math.gcd(seq, R_CHUNK)
    nc = seq // chunk
    dmat, qdec, kdec, cdec = consts
    dk = R_HEADS * R_DK
    const = lambda shape: pl.BlockSpec(shape, lambda b, c: (0,) * len(shape))
    return pl.pallas_call(
        _ret_prompt_kernel,
        grid=(batch, nc),
        in_specs=[
            pl.BlockSpec((chunk, W_RETBLK), lambda b, c: (b * nc + c, Z_RET // W_RETBLK)),
            pl.BlockSpec((dk, chunk), lambda b, c: (0, b * nc + c)),
            pl.BlockSpec((3, chunk, LANE), lambda b, c: (0, c, 0)),
            pl.BlockSpec((2, R_DK // 2, chunk), lambda b, c: (0, 0, c)),
            const((R_HEADS, chunk, chunk)), const((chunk, R_HEADS * R_DV)),
            const((dk, chunk)), const((dk, R_DV)),
            pl.BlockSpec((1, dk, R_DV), lambda b, c: (b, 0, 0)),
        ],
        out_specs=[pl.BlockSpec((chunk, R_HEADS * R_DV), lambda b, c: (b * nc + c, 0)),
                   pl.BlockSpec((1, dk, R_DV), lambda b, c: (b, 0, 0))],
        out_shape=[jax.ShapeDtypeStruct((batch * seq, R_HEADS * R_DV), BF16),
                   jax.ShapeDtypeStruct((batch, dk, R_DV), F32)],
        scratch_shapes=[pltpu.VMEM((dk, R_DV), F32)],
        compiler_params=_params(("parallel", "arbitrary")),
        name="ret_prompt",
    )(z, rkt, rtab, ctab, dmat, qdec, kdec, cdec, s0)


def _ret_sample_kernel(zr_ref, rkt_ref, rtab_ref, ctab_ref, dmat_ref, qdec_ref, kdec_ref, cdec_ref,
                       s0_ref, o_ref, s_ref, ocr_scr, *, dec_seq):
    q, kt, v, rg = _ret_prepare(zr_ref, rkt_ref, rtab_ref, ctab_ref)
    n = q.shape[0]
    nseq = n // dec_seq
    ktb = kt.astype(BF16)
    kd = kt * kdec_ref[...]
    vb = v.astype(BF16)
    qms = [jnp.where(_head_lane_mask(h, n), q, 0.0).astype(BF16) for h in range(R_HEADS)]
    tok = lax.broadcasted_iota(jnp.int32, (R_DK, n), 1) // dec_seq

    for b in range(nseq):
        r0 = b * dec_seq
        sb = s0_ref[b].astype(BF16)
        qstack = jnp.concatenate([qm[r0:r0 + dec_seq] for qm in qms], axis=0)
        res = _dot(qstack, sb)
        for h in range(R_HEADS):
            ocr_scr[r0:r0 + dec_seq, h * R_DV:(h + 1) * R_DV] = res[h * dec_seq:(h + 1) * dec_seq]

    for h in range(R_HEADS):
        vh = vb[:, h * R_DV:(h + 1) * R_DV]
        hs = slice(h * R_DV, (h + 1) * R_DV)
        sc = _dot(qms[h], ktb) * dmat_ref[h]
        o_h = _dot(sc.astype(BF16), vh) + ocr_scr[:, hs] * qdec_ref[:, hs]
        o_ref[:, hs] = _ret_finish(o_h, rg[:, hs])
        rows = slice(h * R_DK, (h + 1) * R_DK)
        kdh = kd[rows]
        for b in range(nseq):
            kb = jnp.where(tok == b, kdh, 0.0).astype(BF16)
            s_ref[b, rows, :] = s0_ref[b, rows, :] * cdec_ref[rows, :] + _dot(kb, vh)


def _ret_sample(z, rkt, rtab, ctab, consts, s0, dec_seq):
    t = z.shape[0]
    n = min(LANE, t)
    nseq = n // dec_seq
    dmat, qdec, kdec, cdec = consts
    dk = R_HEADS * R_DK
    const = lambda shape: pl.BlockSpec(shape, lambda s: (0,) * len(shape))
    return pl.pallas_call(
        functools.partial(_ret_sample_kernel, dec_seq=dec_seq),
        grid=(t // n,),
        in_specs=[
            pl.BlockSpec((n, W_RETBLK), lambda s: (s, Z_RET // W_RETBLK)),
            pl.BlockSpec((dk, n), lambda s: (0, s)),
            pl.BlockSpec((3, n, LANE), lambda s: (0, s, 0)),
            pl.BlockSpec((2, R_DK // 2, n), lambda s: (0, 0, s)),
            const((R_HEADS, n, n)), const((n, R_HEADS * R_DV)),
            const((dk, n)), const((dk, R_DV)),
            pl.BlockSpec((nseq, dk, R_DV), lambda s: (s, 0, 0)),
        ],
        out_specs=[pl.BlockSpec((n, R_HEADS * R_DV), lambda s: (s, 0)),
                   pl.BlockSpec((nseq, dk, R_DV), lambda s: (s, 0, 0))],
        out_shape=[jax.ShapeDtypeStruct((t, R_HEADS * R_DV), BF16),
                   jax.ShapeDtypeStruct((t // dec_seq, dk, R_DV), F32)],
        scratch_shapes=[pltpu.VMEM((n, R_HEADS * R_DV), F32)],
        compiler_params=_params(("parallel",)),
        name="ret_sample",
    )(z, rkt, rtab, ctab, dmat, qdec, kdec, cdec, s0)


def _merge_tail(y_conv, gate_ref, omla_ref, oret_ref, x_ref, wco_ref, wmo_ref, wro_ref, wo_ref, y_ref):
    br_a = _dot(y_conv.astype(BF16), wco_ref[...])
    br_b = _dot(omla_ref[...], wmo_ref[...])
    br_c = _dot(oret_ref[...], wro_ref[...])
    mixed = (jax.nn.sigmoid(gate_ref[:, :D_MODEL]) * br_a
             + jax.nn.sigmoid(gate_ref[:, D_MODEL:2 * D_MODEL]) * br_b
             + jax.nn.sigmoid(gate_ref[:, 2 * D_MODEL:]) * br_c)
    y_ref[...] = x_ref[...] + _dot(mixed.astype(BF16), wo_ref[...])


def _merge_prompt_kernel(zc_ref, gate_ref, omla_ref, oret_ref, x_ref, st_ref, wc_ref,
                         wco_ref, wmo_ref, wro_ref, wo_ref, y_ref, nb_ref, ext_scr):
    tm = zc_ref.shape[0]

    @pl.when(pl.program_id(1) == 0)
    def _():
        ext_scr[0:SUBLANE, :] = st_ref[0]

    cb = zc_ref[:, :CONV_DIM]
    u = zc_ref[:, CONV_DIM:2 * CONV_DIM] * zc_ref[:, 2 * CONV_DIM:]
    ext_scr[SUBLANE:SUBLANE + tm, :] = u
    conv = (ext_scr[SUBLANE - 2:SUBLANE - 2 + tm, :] * wc_ref[0:1, :]
            + ext_scr[SUBLANE - 1:SUBLANE - 1 + tm, :] * wc_ref[1:2, :]
            + u * wc_ref[2:3, :])
    tail = ext_scr[tm:tm + SUBLANE, :]
    ext_scr[0:SUBLANE, :] = tail
    nb_ref[0] = tail
    _merge_tail(cb * conv, gate_ref, omla_ref, oret_ref, x_ref, wco_ref, wmo_ref, wro_ref, wo_ref, y_ref)


def _merge_sample_kernel(zc_ref, gate_ref, omla_ref, oret_ref, x_ref, st0_ref, st1_ref, wc_ref,
                         wco_ref, wmo_ref, wro_ref, wo_ref, y_ref, u_ref, *, dec_seq):
    cb = zc_ref[:, :CONV_DIM]
    u = zc_ref[:, CONV_DIM:2 * CONV_DIM] * zc_ref[:, 2 * CONV_DIM:]
    u_ref[...] = u
    pos = lax.broadcasted_iota(jnp.int32, u.shape, 0) % dec_seq
    um1 = jnp.where(pos == 0, st1_ref[...], pltpu.roll(u, 1, 0))
    um2 = jnp.where(pos == 0, st0_ref[...], jnp.where(pos == 1, st1_ref[...], pltpu.roll(u, 2, 0)))
    conv = um2 * wc_ref[0:1, :] + um1 * wc_ref[1:2, :] + u * wc_ref[2:3, :]
    _merge_tail(cb * conv, gate_ref, omla_ref, oret_ref, x_ref, wco_ref, wmo_ref, wro_ref, wo_ref, y_ref)


def _merge_weight_specs(ngrid):
    const = lambda shape: pl.BlockSpec(shape, lambda *_: (0,) * len(shape))
    return [const((SUBLANE, CONV_DIM)), const((CONV_DIM, D_MODEL)), const((HEADS * VDIM, D_MODEL)),
            const((R_HEADS * R_DV, D_MODEL)), const((D_MODEL, D_MODEL))]


def _merge_prompt(z, omla, oret, x, st_pad, weights, batch, seq):
    tm = min(512, seq)
    nt = seq // tm
    row = lambda b, i: b * nt + i
    return pl.pallas_call(
        _merge_prompt_kernel,
        grid=(batch, nt),
        in_specs=[
            pl.BlockSpec((tm, W_CONVBLK), lambda b, i: (row(b, i), Z_CONV // W_CONVBLK)),
            pl.BlockSpec((tm, 3 * D_MODEL), lambda b, i: (row(b, i), Z_GATE // (3 * D_MODEL))),
            pl.BlockSpec((tm, HEADS * VDIM), lambda b, i: (row(b, i), 0)),
            pl.BlockSpec((tm, R_HEADS * R_DV), lambda b, i: (row(b, i), 0)),
            pl.BlockSpec((tm, D_MODEL), lambda b, i: (row(b, i), 0)),
            pl.BlockSpec((1, SUBLANE, CONV_DIM), lambda b, i: (b, 0, 0)),
        ] + _merge_weight_specs(2),
        out_specs=[pl.BlockSpec((tm, D_MODEL), lambda b, i: (row(b, i), 0)),
                   pl.BlockSpec((1, SUBLANE, CONV_DIM), lambda b, i: (b, 0, 0))],
        out_shape=[jax.ShapeDtypeStruct((batch * seq, D_MODEL), F32),
                   jax.ShapeDtypeStruct((batch, SUBLANE, CONV_DIM), F32)],
        scratch_shapes=[pltpu.VMEM((tm + SUBLANE, CONV_DIM), F32)],
        compiler_params=_params(("parallel", "arbitrary")),
        name="merge_prompt",
    )(z, z, omla, oret, x, st_pad, *weights)


def _merge_sample(z, omla, oret, x, st0, st1, weights, dec_seq):
    t = z.shape[0]
    tm = min(512, t)
    return pl.pallas_call(
        functools.partial(_merge_sample_kernel, dec_seq=dec_seq),
        grid=(t // tm,),
        in_specs=[
            pl.BlockSpec((tm, W_CONVBLK), lambda i: (i, Z_CONV // W_CONVBLK)),
            pl.BlockSpec((tm, 3 * D_MODEL), lambda i: (i, Z_GATE // (3 * D_MODEL))),
            pl.BlockSpec((tm, HEADS * VDIM), lambda i: (i, 0)),
            pl.BlockSpec((tm, R_HEADS * R_DV), lambda i: (i, 0)),
            pl.BlockSpec((tm, D_MODEL), lambda i: (i, 0)),
            pl.BlockSpec((tm, CONV_DIM), lambda i: (i, 0)),
            pl.BlockSpec((tm, CONV_DIM), lambda i: (i, 0)),
        ] + _merge_weight_specs(1),
        out_specs=[pl.BlockSpec((tm, D_MODEL), lambda i: (i, 0)),
                   pl.BlockSpec((tm, CONV_DIM), lambda i: (i, 0))],
        out_shape=[jax.ShapeDtypeStruct((t, D_MODEL), F32),
                   jax.ShapeDtypeStruct((t, CONV_DIM), F32)],
        compiler_params=_params(("parallel",)),
        name="merge_sample",
    )(z, z, omla, oret, x, st0, st1, *weights)


def _ffn_kernel(x_ref, g_ref, wu_ref, wd_ref, y_ref, h_scr, acc_scr):
    j = pl.program_id(1)

    @pl.when(j == 0)
    def _():
        x = x_ref[...]
        ms = jnp.mean(x * x, axis=-1, keepdims=True)
        h_scr[...] = (x * lax.rsqrt(ms + EPS) * g_ref[...]).astype(BF16)
        acc_scr[...] = x

    a = jnp.maximum(_dot(h_scr[...], wu_ref[...]), 0.0)
    acc_scr[...] += _dot((a * a).astype(BF16), wd_ref[...])

    @pl.when(j == pl.num_programs(1) - 1)
    def _():
        y_ref[...] = acc_scr[...]


def _ffn(x, g, wu, wd):
    t = x.shape[0]
    tm = min(1024, t)
    tf = 1024
    return pl.pallas_call(
        _ffn_kernel,
        grid=(t // tm, D_FF // tf),
        in_specs=[
            pl.BlockSpec((tm, D_MODEL), lambda i, j: (i, 0)),
            pl.BlockSpec((1, D_MODEL), lambda i, j: (0, 0)),
            pl.BlockSpec((D_MODEL, tf), lambda i, j: (0, j)),
            pl.BlockSpec((tf, D_MODEL), lambda i, j: (j, 0)),
        ],
        out_specs=pl.BlockSpec((tm, D_MODEL), lambda i, j: (i, 0)),
        out_shape=jax.ShapeDtypeStruct((t, D_MODEL), F32),
        scratch_shapes=[pltpu.VMEM((tm, D_MODEL), BF16), pltpu.VMEM((tm, D_MODEL), F32)],
        compiler_params=_params(("parallel", "arbitrary")),
        name="ffn",
    )(x, g, wu, wd)


def _rope_tables(pos):
    posf = pos.astype(F32)[:, None]
    n = pos.shape[0]

    def cs(half):
        inv = THETA ** (-jnp.arange(half, dtype=F32) / half)
        ang = posf * inv[None, :]
        return jnp.cos(ang), jnp.sin(ang)

    c16, s16 = cs(ROPE // 2)
    one = lambda w: jnp.ones((n, w), F32)
    zero = lambda w: jnp.zeros((n, w), F32)
    qtab = jnp.stack([
        jnp.concatenate([one(NOPE), c16, c16, one(HT - QK)], axis=1),
        jnp.concatenate([zero(NOPE), -s16, zero(HT - NOPE - ROPE // 2)], axis=1),
        jnp.concatenate([zero(NOPE + ROPE // 2), s16, zero(HT - QK)], axis=1)])
    ktab = jnp.stack([
        jnp.concatenate([c16, c16, one(LANE - ROPE)], axis=1),
        jnp.concatenate([-s16, zero(LANE - ROPE // 2)], axis=1),
        jnp.concatenate([zero(ROPE // 2), s16, zero(LANE - ROPE)], axis=1)])
    c32, s32 = cs(R_DK // 2)
    z32 = zero(R_DK // 2)
    rtab = jnp.stack([
        jnp.concatenate([c32] * (LANE // (R_DK // 2)), axis=1),
        jnp.concatenate([-s32, z32] * (LANE // R_DK), axis=1),
        jnp.concatenate([z32, s32] * (LANE // R_DK), axis=1)])
    ctab = jnp.stack([c32.T, s32.T])
    return qtab, ktab, rtab, ctab


def _ret_consts(n, chunk):
    log_g = jnp.log1p(-(2.0 ** (-5.0 - jnp.arange(R_HEADS, dtype=F32))))
    idx = jnp.arange(n)
    loc = (idx % chunk).astype(F32)
    same = (idx[:, None] // chunk) == (idx[None, :] // chunk)
    diff = loc[:, None] - loc[None, :]
    dmat = jnp.where(same & (diff >= 0), jnp.exp(log_g[:, None, None] * jnp.maximum(diff, 0.0)), 0.0)
    qd = jnp.exp(log_g[None, :] * (loc[:, None] + 1.0))
    qdec = jnp.repeat(qd, R_DV, axis=1)
    kd = jnp.exp(log_g[:, None] * (chunk - 1.0 - loc[None, :]))
    kdec = jnp.repeat(kd, R_DK, axis=0)
    cdec = jnp.repeat(jnp.exp(log_g * chunk)[:, None], R_DK, axis=0) * jnp.ones((1, R_DV), F32)
    return dmat.astype(F32), qdec, kdec, cdec


def _layer_weights(l, g_mix, w_in, w_conv, w_conv_out, g_q_lat, w_uq, g_kv_lat, w_ukv, g_qn, g_kn,
                   w_mla_out, w_ret_out, w_o, g_ffn, w_up, w_down):
    wi = w_in[l]
    o_cq, o_rq, o_rk, o_gl, o_end = 1536, 2208, 2464, 3744, 6816
    w_perm = jnp.concatenate(
        [wi[:, :o_cq], wi[:, o_rq:o_gl], wi[:, o_gl:o_end], wi[:, o_cq:o_rq],
         jnp.zeros((D_MODEL, LANE - ROPE), F32)], axis=1).astype(BF16)
    w_rk_t = wi[:, o_rk:o_rk + R_HEADS * R_DK].T.astype(BF16)
    wuq = jnp.pad(w_uq[l].reshape(Q_LORA, HEADS, QK), ((0, 0), (0, 0), (0, HT - QK)))
    wuq = wuq.reshape(Q_LORA, HEADS * HT).astype(BF16)
    wkv = w_ukv[l].reshape(KV_LORA, HEADS, NOPE + VDIM)
    w_uk, w_uv = wkv[..., :NOPE], wkv[..., NOPE:]
    wk_pad = jnp.pad(w_uk, ((0, 0), (0, 0), (0, HT - NOPE))).reshape(KV_LORA, HEADS * HT).astype(BF16)
    wv = w_uv.reshape(KV_LORA, HEADS * VDIM).astype(BF16)
    wukt = w_uk.reshape(KV_LORA, HEADS * NOPE).T.astype(BF16)
    j = np.arange(ROPE)
    pm = np.zeros((LANE, HEADS * HT), np.float32)
    for h in range(HEADS):
        pm[j, h * HT + NOPE + j] = 1.0
    shift = np.zeros((HT, LANE), np.float32)
    shift[NOPE + j, j] = 1.0
    wabs = jnp.pad(jnp.transpose(w_uk, (1, 2, 0)), ((0, 0), (0, HT - NOPE), (0, 0)))
    wcomb = jnp.concatenate([wabs, jnp.broadcast_to(jnp.asarray(shift), (HEADS, HT, LANE))],
                            axis=2).astype(BF16)
    pad_gain = lambda g: jnp.pad(g, (0, HT - QK))[None, :]
    return dict(
        g_mix=g_mix[l][None, :], w_perm=w_perm, w_rk_t=w_rk_t,
        g_q_lat=g_q_lat[l][None, :], wuq=wuq, g_kv_lat=g_kv_lat[l][None, :],
        gqn=pad_gain(g_qn[l]) * (QK ** -0.5), gkn=pad_gain(g_kn[l]),
        wk_pad=wk_pad, pmat=jnp.asarray(pm, BF16), wv=wv, wukt=wukt, wcomb=wcomb,
        merge=(jnp.pad(w_conv[l], ((0, SUBLANE - CONV_W), (0, 0))), w_conv_out[l].astype(BF16),
               w_mla_out[l].astype(BF16), w_ret_out[l].astype(BF16), w_o[l].astype(BF16)),
        g_ffn=g_ffn[l][None, :], w_up=w_up[l].astype(BF16), w_down=w_down[l].astype(BF16))


def kernel(x_prompt, x_sample, cache_ckv, cache_kpe, state_conv, state_ret, page_table, g_mix, w_in,
           w_conv, w_conv_out, g_q_lat, w_uq, g_kv_lat, w_ukv, g_qn, g_kn, w_mla_out, w_ret_out, w_o,
           g_ffn, w_up, w_down):
    batch, seq, _ = x_prompt.shape
    nb, dec_seq, _ = x_sample.shape
    depth = w_in.shape[0]
    past = page_table.shape[1] * PAGE
    assert dec_seq == SUBLANE
    tp, ts = batch * seq, nb * dec_seq
    dk = R_HEADS * R_DK

    tabs_p = _rope_tables(jnp.arange(seq))
    tabs_s = _rope_tables(jnp.tile(past + jnp.arange(dec_seq), nb))
    chunk_p = math.gcd(seq, R_CHUNK)
    consts_p = _ret_consts(chunk_p, chunk_p)
    consts_s = _ret_consts(min(LANE, ts), dec_seq)
    tm_p, tm_s = min(512, seq), min(512, ts)

    yp = x_prompt.reshape(tp, D_MODEL)
    ys = x_sample.reshape(ts, D_MODEL)
    conv0 = jnp.zeros((batch, SUBLANE, CONV_DIM), F32)
    ret0 = jnp.zeros((batch, dk, R_DV), F32)
    outs = {k: [] for k in ("ckv_p", "kpe_p", "conv_p", "ret_p", "ckv_s", "kpe_s", "conv_s", "ret_s")}

    for l in range(depth):
        w = _layer_weights(l, g_mix, w_in, w_conv, w_conv_out, g_q_lat, w_uq, g_kv_lat, w_ukv, g_qn,
                           g_kn, w_mla_out, w_ret_out, w_o, g_ffn, w_up, w_down)

        qtab, ktab, rtab, ctab = tabs_p
        z, rkt = _inproj(yp, w["g_mix"], w["w_perm"], w["w_rk_t"])
        q, ckvn, kper, k, v = _mlaprep(z, qtab, ktab, tm_p, w["g_q_lat"], w["wuq"], w["g_kv_lat"],
                                       w["gqn"], (w["wk_pad"], w["pmat"], w["wv"], w["gkn"] ))
        omla = _flash(q, k, v, batch, seq)
        oret, snew = _ret_prompt(z, rkt, rtab, ctab, consts_p, ret0, batch, seq)
        ymid, nbuf = _merge_prompt(z, omla, oret, yp, conv0, w["merge"], batch, seq)
        yp = _ffn(ymid, w["g_ffn"], w["w_up"], w["w_down"])
        outs["ckv_p"].append(ckvn.reshape(batch, seq, KV_LORA))
        outs["kpe_p"].append(kper.reshape(batch, seq, ROPE))
        outs["conv_p"].append(nbuf[:, SUBLANE - (CONV_W - 1):, :])
        outs["ret_p"].append(snew.reshape(batch, R_HEADS, R_DK, R_DV))

        qtab, ktab, rtab, ctab = tabs_s
        z, rkt = _inproj(ys, w["g_mix"], w["w_perm"], w["w_rk_t"])
        q, ckvn, kper = _mlaprep(z, qtab, ktab, tm_s, w["g_q_lat"], w["wuq"], w["g_kv_lat"],
                                 w["gqn"], None)
        qa, qp = _qabs(q, w["gkn"], w["wcomb"], nb, dec_seq)
        cnew = jnp.pad(ckvn.reshape(nb, dec_seq, KV_LORA), ((0, 0), (0, PAGE - dec_seq), (0, 0)))
        pnew = jnp.pad(kper.reshape(nb, dec_seq, ROPE), ((0, 0), (0, PAGE - dec_seq), (0, 0)))
        omla = _decode(page_table, l, cache_ckv, cache_kpe,
                       qa.reshape(nb, HEADS * dec_seq, KV_LORA), qp.reshape(nb, HEADS * dec_seq, LANE),
                       w["wukt"], w["wv"], cnew, pnew).reshape(ts, HEADS * VDIM)
        oret, snew = _ret_sample(z, rkt, rtab, ctab, consts_s, state_ret[l].reshape(nb, dk, R_DV), dec_seq)
        st = state_conv[l]
        st0 = jnp.repeat(st[:, 0, :], dec_seq, axis=0)
        st1 = jnp.repeat(st[:, 1, :], dec_seq, axis=0)
        ymid, u = _merge_sample(z, omla, oret, ys, st0, st1, w["merge"], dec_seq)
        ys = _ffn(ymid, w["g_ffn"], w["w_up"], w["w_down"])
        outs["ckv_s"].append(ckvn.reshape(nb, dec_seq, KV_LORA))
        outs["kpe_s"].append(kper.reshape(nb, dec_seq, ROPE))
        outs["conv_s"].append(u.reshape(nb, dec_seq, CONV_DIM)[:, dec_seq - (CONV_W - 1):, :])
        outs["ret_s"].append(snew.reshape(nb, R_HEADS, R_DK, R_DV))

    return (yp.reshape(batch, seq, D_MODEL), ys.reshape(nb, dec_seq, D_MODEL),
            jnp.stack(outs["ckv_p"]), jnp.stack(outs["kpe_p"]), jnp.stack(outs["conv_p"]),
            jnp.stack(outs["ret_p"]),
            jnp.stack(outs["ckv_s"]), jnp.stack(outs["kpe_s"]), jnp.stack(outs["conv_s"]),
            jnp.stack(outs["ret_s"]))
```

```python
import functools
import math

import jax
import jax.numpy as jnp
import numpy as np
from jax import lax
from jax.experimental import pallas as pl
from jax.experimental.pallas import tpu as pltpu

F32 = jnp.float32
BF16 = jnp.bfloat16

D_MODEL = 1024
PAGE = 128
CONV_DIM = 512
CONV_W = 3
HEADS = 8
NOPE = 64
ROPE = 32
QK = NOPE + ROPE
VDIM = 64
Q_LORA = 384
KV_LORA = 256
R_HEADS = 4
R_DK = 64
R_DV = 128
R_CHUNK = 128
D_FF = 4096
THETA = 10000.0
EPS = 1e-6
NEG = -1e30

LANE = 128
SUBLANE = 8
HT = LANE

Z_CONV = 0
Z_RET = 1536
Z_GATE = 3072
Z_MLA = 6144
Z_W = 6912
W_CONVBLK = 3 * CONV_DIM
W_RETBLK = 2 * R_HEADS * R_DK + 2 * R_HEADS * R_DV
W_MLABLK = Q_LORA + KV_LORA + LANE

VMEM_LIMIT = 56 * 1024 * 1024


def _dot(a, b):
    return jnp.dot(a, b, preferred_element_type=F32)


def _dot_nt(a, b):
    return lax.dot_general(a, b, (((1,), (1,)), ((), ())), preferred_element_type=F32)


def _params(sem):
    return pltpu.CompilerParams(dimension_semantics=sem, vmem_limit_bytes=VMEM_LIMIT)


def _rope_rows(x, tab_ref, half):
    return (x * tab_ref[0] + pltpu.roll(x, LANE - half, 1) * tab_ref[1]
            + pltpu.roll(x, half, 1) * tab_ref[2])


def _inproj_kernel(x_ref, g_ref, w_ref, wrk_ref, z_ref, rkt_ref, h_scr):
    @pl.when(pl.program_id(1) == 0)
    def _():
        x = x_ref[...]
        ms = jnp.mean(x * x, axis=-1, keepdims=True)
        hb = (x * lax.rsqrt(ms + EPS) * g_ref[...]).astype(BF16)
        h_scr[...] = hb
        rkt_ref[...] = _dot_nt(wrk_ref[...], hb)

    z_ref[...] = _dot(h_scr[...], w_ref[...])


def _inproj(x, g, w_perm, w_rk_t):
    t = x.shape[0]
    tm = min(1024, t)
    tn = 2304
    return pl.pallas_call(
        _inproj_kernel,
        grid=(t // tm, Z_W // tn),
        in_specs=[
            pl.BlockSpec((tm, D_MODEL), lambda i, j: (i, 0)),
            pl.BlockSpec((1, D_MODEL), lambda i, j: (0, 0)),
            pl.BlockSpec((D_MODEL, tn), lambda i, j: (0, j)),
            pl.BlockSpec((R_HEADS * R_DK, D_MODEL), lambda i, j: (0, 0)),
        ],
        out_specs=[
            pl.BlockSpec((tm, tn), lambda i, j: (i, j)),
            pl.BlockSpec((R_HEADS * R_DK, tm), lambda i, j: (0, i)),
        ],
        out_shape=[jax.ShapeDtypeStruct((t, Z_W), F32),
                   jax.ShapeDtypeStruct((R_HEADS * R_DK, t), F32)],
        scratch_shapes=[pltpu.VMEM((tm, D_MODEL), BF16)],
        compiler_params=_params(("parallel", "arbitrary")),
        name="inproj",
    )(x, g, w_perm, w_rk_t)


def _mlaprep_kernel(z_ref, qtab_ref, ktab_ref, gq_ref, wuq_ref, gkv_ref, gqn_ref,
                    *rest, expand):
    if expand:
        wk_ref, pmat_ref, wv_ref, gkn_ref, q_ref, ckv_ref, kpe_ref, k_ref, v_ref = rest
    else:
        q_ref, ckv_ref, kpe_ref = rest
    cq = z_ref[:, :Q_LORA]
    ckv = z_ref[:, Q_LORA:Q_LORA + KV_LORA]
    kpe = z_ref[:, Q_LORA + KV_LORA:]

    cqn = cq * lax.rsqrt(jnp.mean(cq * cq, axis=-1, keepdims=True) + EPS) * gq_ref[...]
    qf = _dot(cqn.astype(BF16), wuq_ref[...])
    for h in range(HEADS):
        xr = _rope_rows(qf[:, h * HT:(h + 1) * HT], qtab_ref, ROPE // 2)
        ms = jnp.sum(xr * xr, axis=-1, keepdims=True) * (1.0 / QK)
        q_ref[:, h * HT:(h + 1) * HT] = (xr * lax.rsqrt(ms + EPS) * gqn_ref[...]).astype(BF16)

    ckvn = ckv * lax.rsqrt(jnp.mean(ckv * ckv, axis=-1, keepdims=True) + EPS) * gkv_ref[...]
    ckv_ref[...] = ckvn
    kper = _rope_rows(kpe, ktab_ref, ROPE // 2)
    kpe_ref[...] = kper[:, :ROPE]

    if expand:
        cb = ckvn.astype(BF16)
        p_hi = kper.astype(BF16)
        p_lo = (kper - p_hi.astype(F32)).astype(BF16)
        kf = _dot(cb, wk_ref[...]) + _dot(p_hi, pmat_ref[...]) + _dot(p_lo, pmat_ref[...])
        for h in range(HEADS):
            xk = kf[:, h * HT:(h + 1) * HT]
            ms = jnp.sum(xk * xk, axis=-1, keepdims=True) * (1.0 / QK)
            k_ref[:, h * HT:(h + 1) * HT] = (xk * lax.rsqrt(ms + EPS) * gkn_ref[...]).astype(BF16)
        vt = _dot_nt(wv_ref[...], cb)
        row = lax.broadcasted_iota(jnp.int32, vt.shape, 0) % HT
        v_ref[...] = jnp.where(row == VDIM, 1.0, vt).astype(BF16)


def _mlaprep(z, qtab, ktab, tm, gq, wuq, gkv, gqn, expand_args):
    t = z.shape[0]
    n_seq_tiles = qtab.shape[1] // tm
    expand = expand_args is not None
    const = lambda shape: pl.BlockSpec(shape, lambda i: (0,) * len(shape))
    tab = pl.BlockSpec((3, tm, LANE), lambda i: (0, i % n_seq_tiles, 0))
    in_specs = [
        pl.BlockSpec((tm, W_MLABLK), lambda i: (i, Z_MLA // W_MLABLK)),
        tab, tab,
        const((1, Q_LORA)), const((Q_LORA, HEADS * HT)), const((1, KV_LORA)), const((1, HT)),
    ]
    args = [z, qtab, ktab, gq, wuq, gkv, gqn]
    out_specs = [pl.BlockSpec((tm, HEADS * HT), lambda i: (i, 0)),
                 pl.BlockSpec((tm, KV_LORA), lambda i: (i, 0)),
                 pl.BlockSpec((tm, ROPE), lambda i: (i, 0))]
    out_shape = [jax.ShapeDtypeStruct((t, HEADS * HT), BF16),
                 jax.ShapeDtypeStruct((t, KV_LORA), F32),
                 jax.ShapeDtypeStruct((t, ROPE), F32)]
    if expand:
        wk, pmat, wv, gkn = expand_args
        in_specs += [const((KV_LORA, HEADS * HT)), const((LANE, HEADS * HT)),
                     const((HEADS * HT, KV_LORA)), const((1, HT))]
        args += [wk, pmat, wv, gkn]
        out_specs += [pl.BlockSpec((tm, HEADS * HT), lambda i: (i, 0)),
                      pl.BlockSpec((HEADS * HT, tm), lambda i: (0, i))]
        out_shape += [jax.ShapeDtypeStruct((t, HEADS * HT), BF16),
                      jax.ShapeDtypeStruct((HEADS * HT, t), BF16)]
    return pl.pallas_call(
        functools.partial(_mlaprep_kernel, expand=expand),
        grid=(t // tm,),
        in_specs=in_specs, out_specs=out_specs, out_shape=out_shape,
        compiler_params=_params(("parallel",)),
        name="mlaprep_expand" if expand else "mlaprep",
    )(*args)


def _flash_kernel(q_ref, k_ref, vt_ref, o_ref, sa_scr, sb_scr, m_scr, acc_scr, *, tq, tk):
    qi = pl.program_id(2)
    m_scr[...] = jnp.full(m_scr.shape, NEG, F32)
    acc_scr[...] = jnp.zeros(acc_scr.shape, F32)

    def scores(ki, s_scr):
        start = pl.multiple_of(ki * tk, tk)
        for h in range(2):
            hs = slice(h * HT, (h + 1) * HT)
            s_scr[h] = _dot_nt(k_ref[pl.ds(start, tk), hs], q_ref[:, hs])

    def attend(ki, s_scr, diag):
        start = pl.multiple_of(ki * tk, tk)
        for h in range(2):
            s = s_scr[h]
            if diag is not None:
                key = lax.broadcasted_iota(jnp.int32, (tk, tq), 0) + diag * tk
                qry = lax.broadcasted_iota(jnp.int32, (tk, tq), 1)
                s = jnp.where(key <= qry, s, NEG)
            m = m_scr[h]
            m_new = jnp.maximum(m, jnp.max(s, axis=0, keepdims=True))
            p = jnp.exp(s - m_new).astype(BF16)
            vt = vt_ref[h * HT:(h + 1) * HT, pl.ds(start, tk)]
            acc_scr[h] = jnp.exp(m - m_new) * acc_scr[h] + _dot(vt, p)
            m_scr[h] = m_new

    def body(j, carry):
        scores(2 * j + 1, sb_scr)
        attend(2 * j, sa_scr, None)
        scores(2 * j + 2, sa_scr)
        attend(2 * j + 1, sb_scr, None)
        return carry

    scores(0, sa_scr)
    lax.fori_loop(0, qi, body, 0)
    scores(2 * qi + 1, sb_scr)
    attend(2 * qi, sa_scr, 0)
    attend(2 * qi + 1, sb_scr, 1)
    outs = []
    for h in range(2):
        a = acc_scr[h]
        outs.append((a / a[VDIM:VDIM + 1, :]).T)
    lane = lax.broadcasted_iota(jnp.int32, (tq, LANE), 1)
    o_ref[...] = jnp.where(lane < VDIM, outs[0], pltpu.roll(outs[1], VDIM, 1)).astype(BF16)


def _flash(q, k, vt, batch, seq):
    tq = min(512, seq)
    tk = tq // 2
    nq = seq // tq
    return pl.pallas_call(
        functools.partial(_flash_kernel, tq=tq, tk=tk),
        grid=(batch, HEADS // 2, nq),
        in_specs=[
            pl.BlockSpec((tq, 2 * HT), lambda b, hp, i: (b * nq + i, hp)),
            pl.BlockSpec((seq, 2 * HT), lambda b, hp, i: (b, hp)),
            pl.BlockSpec((2 * HT, seq), lambda b, hp, i: (hp, b)),
        ],
        out_specs=pl.BlockSpec((tq, 2 * VDIM), lambda b, hp, i: (b * nq + i, hp)),
        out_shape=jax.ShapeDtypeStruct((batch * seq, HEADS * VDIM), BF16),
        scratch_shapes=[pltpu.VMEM((2, tk, tq), F32), pltpu.VMEM((2, tk, tq), F32),
                        pltpu.VMEM((2, 1, tq), F32), pltpu.VMEM((2, HT, tq), F32)],
        compiler_params=_params(("parallel", "parallel", "arbitrary")),
        name="flash",
    )(q, k, vt)


def _qabs_kernel(q_ref, gkn_ref, wcomb_ref, qa_ref, qp_ref):
    nb = qa_ref.shape[0]
    for h in range(HEADS):
        qg = (q_ref[:, h * HT:(h + 1) * HT].astype(F32) * gkn_ref[...]).astype(BF16)
        r = _dot(qg, wcomb_ref[h])
        qa_ref[:, h] = r[:, :KV_LORA].reshape(nb, SUBLANE, KV_LORA)
        qp_ref[:, h] = r[:, KV_LORA:].reshape(nb, SUBLANE, LANE)


def _qabs(q, gkn, wcomb, nb, dec_seq):
    t = q.shape[0]
    return pl.pallas_call(
        _qabs_kernel,
        grid=(1,),
        in_specs=[pl.BlockSpec((t, HEADS * HT), lambda i: (0, 0)),
                  pl.BlockSpec((1, HT), lambda i: (0, 0)),
                  pl.BlockSpec((HEADS, HT, KV_LORA + LANE), lambda i: (0, 0, 0))],
        out_specs=[pl.BlockSpec((nb, HEADS, dec_seq, KV_LORA), lambda i: (0, 0, 0, 0)),
                   pl.BlockSpec((nb, HEADS, dec_seq, LANE), lambda i: (0, 0, 0, 0))],
        out_shape=[jax.ShapeDtypeStruct((nb, HEADS, dec_seq, KV_LORA), F32),
                   jax.ShapeDtypeStruct((nb, HEADS, dec_seq, LANE), F32)],
        compiler_params=_params(("arbitrary",)),
        name="qabs",
    )(q, gkn, wcomb)


DEC_ROWS = HEADS * SUBLANE
DEC_KROWS = 3 * ROPE + ROPE
DEC_PROWS = DEC_ROWS + 16


def _decode_kernel(pt_ref, qa_ref, qp_ref, wukt_ref, wv_ref, cnew_ref, pnew_ref, ckv_hbm, kpe_hbm,
                   o_ref, cbuf, pbuf, sem, cbf, kt, wall, lpe, s_scr, m_scr, l_scr, acc_scr,
                   *, npg, sub, grp, layer):
    b = pl.program_id(0)
    g = pl.program_id(1)
    ng = pl.num_programs(1)
    step = b * ng + g
    slot = step % 2
    nk = HEADS * NOPE

    def page_copies(bb, gg, sl):
        copies = []
        for i in range(npg):
            pid = 0 if bb is None else pt_ref[bb, gg * npg + i]
            copies.append(pltpu.make_async_copy(ckv_hbm.at[layer, pid], cbuf.at[sl, i], sem.at[0, sl]))
            copies.append(pltpu.make_async_copy(kpe_hbm.at[layer, pid], pbuf.at[sl, i], sem.at[1, sl]))
        return copies

    @pl.when(step == 0)
    def _():
        for cp in page_copies(b, g, slot):
            cp.start()

    for cp in page_copies(None, None, slot):
        cp.wait()
    c_pages = [cbuf.at[slot, i] for i in range(npg)]
    p_pages = [pbuf.at[slot, i] for i in range(npg)]

    @pl.when(g == 0)
    def _():
        m_scr[...] = jnp.full(m_scr.shape, NEG, F32)
        l_scr[...] = jnp.zeros(l_scr.shape, F32)
        acc_scr[...] = jnp.zeros(acc_scr.shape, F32)
        kt[3 * ROPE:, :] = jnp.zeros((DEC_KROWS - 3 * ROPE, kt.shape[1]), BF16)
        wall[:nk, :] = wukt_ref[...]
        wall[nk:, :] = qa_ref[0].astype(BF16)
        lane = lax.broadcasted_iota(jnp.int32, (DEC_PROWS - DEC_ROWS, DEC_KROWS), 1)
        lpe[:DEC_ROWS, :] = qp_ref[0].astype(BF16)
        lpe[DEC_ROWS:, :] = jnp.where((lane >= ROPE) & (lane < 3 * ROPE), 1.0, 0.0).astype(BF16)

    def stage(i, c, kp_t):
        cols = slice(i * PAGE, (i + 1) * PAGE)
        cbf[cols, :] = c.astype(BF16)
        kt[:ROPE, cols] = kp_t.astype(BF16)
        kp2 = kp_t * kp_t
        hi = kp2.astype(BF16)
        kt[ROPE:2 * ROPE, cols] = hi
        kt[2 * ROPE:3 * ROPE, cols] = (kp2 - hi.astype(F32)).astype(BF16)

    def scores(start, width):
        ct = cbf[start:start + width, :]
        big = _dot_nt(wall[...], ct)
        ext = _dot(lpe[...], kt[:, start:start + width])
        kpsq = ext[DEC_ROWS:DEC_ROWS + 1]
        for h in range(HEADS):
            blk = big[h * NOPE:(h + 1) * NOPE]
            nsq = jnp.sum(blk * blk, axis=0, keepdims=True)
            rs = lax.rsqrt((nsq + kpsq) * (1.0 / QK) + EPS)
            hr = slice(h * SUBLANE, (h + 1) * SUBLANE)
            s_scr[hr, start:start + width] = (big[nk + h * SUBLANE:nk + (h + 1) * SUBLANE] + ext[hr]) * rs

    def update(width, mask):
        if mask is not None:
            s_scr[:, :width] = jnp.where(mask, s_scr[:, :width], NEG)
        m = m_scr[...]
        m_new = jnp.maximum(m, jnp.max(s_scr[:, :width], axis=-1, keepdims=True))
        alpha = jnp.exp(m - m_new)
        acc = alpha * acc_scr[...]
        lsum = alpha * l_scr[...]
        chunk = min(grp, width)
        for c in range(width // chunk):
            cs = slice(c * chunk, (c + 1) * chunk)
            p = jnp.exp(s_scr[:, cs] - m_new)
            lsum = lsum + jnp.sum(p, axis=-1, keepdims=True)
            acc = acc + _dot(p.astype(BF16), cbf[cs, :])
        l_scr[...] = lsum
        acc_scr[...] = acc
        m_scr[...] = m_new

    pages_per_sub = sub // PAGE
    for j in range(npg // pages_per_sub):
        for i in range(j * pages_per_sub, (j + 1) * pages_per_sub):
            stage(i, c_pages[i][...], p_pages[i][...])
        scores(j * sub, sub)

    wrap = g + 1 == ng
    b_next = jnp.where(wrap, jnp.where(b + 1 == pl.num_programs(0), 0, b + 1), b)
    prefetch = page_copies(b_next, jnp.where(wrap, 0, g + 1), 1 - slot)
    for cp in prefetch:
        cp.start()

    update(npg * PAGE, None)

    @pl.when(g == ng - 1)
    def _():
        stage(0, cnew_ref[0], pnew_ref[0])
        scores(0, PAGE)
        r = lax.broadcasted_iota(jnp.int32, (DEC_ROWS, PAGE), 0)
        t = lax.broadcasted_iota(jnp.int32, (DEC_ROWS, PAGE), 1)
        update(PAGE, t <= (r % SUBLANE))
        olat = (acc_scr[...] / l_scr[...]).astype(BF16)
        full = _dot(olat, wv_ref[...])
        colh = lax.broadcasted_iota(jnp.int32, (SUBLANE, HEADS * VDIM), 1) // VDIM
        out = jnp.zeros((SUBLANE, HEADS * VDIM), F32)
        for h in range(HEADS):
            out = out + jnp.where(colh == h, full[h * SUBLANE:(h + 1) * SUBLANE], 0.0)
        o_ref[0] = out.astype(BF16)

    @pl.when(step == pl.num_programs(0) * ng - 1)
    def _():
        for cp in prefetch:
            cp.wait()


def _decode(page_table, layer, cache_ckv, cache_kpe_t, qa, qp, wukt, wv, cnew, pnew_t):
    nb, n_pages = page_table.shape
    npg = min(16, n_pages)
    sub = min(256, npg * PAGE)
    ngrp = n_pages // npg

    in_specs = [
        pl.BlockSpec((1, DEC_ROWS, KV_LORA), lambda b, g, pt: (b, 0, 0)),
        pl.BlockSpec((1, DEC_ROWS, DEC_KROWS), lambda b, g, pt: (b, 0, 0)),
        pl.BlockSpec((HEADS * NOPE, KV_LORA), lambda b, g, pt: (0, 0)),
        pl.BlockSpec((KV_LORA, HEADS * VDIM), lambda b, g, pt: (0, 0)),
        pl.BlockSpec((1, PAGE, KV_LORA), lambda b, g, pt: (b, 0, 0)),
        pl.BlockSpec((1, ROPE, PAGE), lambda b, g, pt: (b, 0, 0)),
        pl.BlockSpec(memory_space=pl.ANY),
        pl.BlockSpec(memory_space=pl.ANY),
    ]
    grid_spec = pltpu.PrefetchScalarGridSpec(
        num_scalar_prefetch=1,
        grid=(nb, ngrp),
        in_specs=in_specs,
        out_specs=pl.BlockSpec((1, SUBLANE, HEADS * VDIM), lambda b, g, pt: (b, 0, 0)),
        scratch_shapes=[
            pltpu.VMEM((2, npg, PAGE, KV_LORA), F32),
            pltpu.VMEM((2, npg, ROPE, PAGE), F32),
            pltpu.SemaphoreType.DMA((2, 2)),
            pltpu.VMEM((npg * PAGE, KV_LORA), BF16),
            pltpu.VMEM((DEC_KROWS, npg * PAGE), BF16),
            pltpu.VMEM((HEADS * NOPE + DEC_ROWS, KV_LORA), BF16),
            pltpu.VMEM((DEC_PROWS, DEC_KROWS), BF16),
            pltpu.VMEM((DEC_ROWS, npg * PAGE), F32),
            pltpu.VMEM((DEC_ROWS, 1), F32),
            pltpu.VMEM((DEC_ROWS, 1), F32),
            pltpu.VMEM((DEC_ROWS, KV_LORA), F32),
        ],
    )
    return pl.pallas_call(
        functools.partial(_decode_kernel, npg=npg, sub=sub, grp=min(512, npg * PAGE), layer=layer),
        grid_spec=grid_spec,
        out_shape=jax.ShapeDtypeStruct((nb, SUBLANE, HEADS * VDIM), BF16),
        compiler_params=_params(("arbitrary", "arbitrary")),
        name="decode",
    )(page_table, qa, qp, wukt, wv, cnew, pnew_t, cache_ckv, cache_kpe_t)


def _ret_prepare(zr_ref, rkt_ref, rtab_ref, ctab_ref):
    dk = R_HEADS * R_DK
    q = jnp.concatenate(
        [_rope_rows(zr_ref[:, i * LANE:(i + 1) * LANE], rtab_ref, R_DK // 2) for i in range(dk // LANE)],
        axis=1)
    cos_t = ctab_ref[0]
    sin_t = ctab_ref[1]
    half = R_DK // 2
    kparts = []
    for h in range(R_HEADS):
        a = rkt_ref[h * R_DK:h * R_DK + half, :]
        b = rkt_ref[h * R_DK + half:(h + 1) * R_DK, :]
        kparts += [a * cos_t - b * sin_t, b * cos_t + a * sin_t]
    kt = jnp.concatenate(kparts, axis=0) * (R_DK ** -0.5)
    v = zr_ref[:, 2 * dk:2 * dk + R_HEADS * R_DV]
    rg = zr_ref[:, 2 * dk + R_HEADS * R_DV:]
    return q, kt, v, rg


def _ret_finish(o_h, rg_h):
    ms = jnp.mean(o_h * o_h, axis=-1, keepdims=True)
    return (jax.nn.silu(rg_h) * (o_h * lax.rsqrt(ms + EPS))).astype(BF16)


def _head_lane_mask(h, n):
    lane = lax.broadcasted_iota(jnp.int32, (n, R_HEADS * R_DK), 1)
    return (lane // R_DK) == h


def _ret_prompt_kernel(zr_ref, rkt_ref, rtab_ref, ctab_ref, dmat_ref, qdec_ref, kdec_ref, cdec_ref,
                       s0_ref, o_ref, s_ref, s_scr):
    c = pl.program_id(1)

    @pl.when(c == 0)
    def _():
        s_scr[...] = s0_ref[0]

    q, kt, v, rg = _ret_prepare(zr_ref, rkt_ref, rtab_ref, ctab_ref)
    n = q.shape[0]
    ktb = kt.astype(BF16)
    kdb = (kt * kdec_ref[...]).astype(BF16)
    vb = v.astype(BF16)
    s_old = s_scr[...]
    sb = s_old.astype(BF16)
    for h in range(R_HEADS):
        qm = jnp.where(_head_lane_mask(h, n), q, 0.0).astype(BF16)
        vh = vb[:, h * R_DV:(h + 1) * R_DV]
        sc = _dot(qm, ktb) * dmat_ref[h]
        o_h = _dot(sc.astype(BF16), vh) + _dot(qm, sb) * qdec_ref[:, h * R_DV:(h + 1) * R_DV]
        o_ref[:, h * R_DV:(h + 1) * R_DV] = _ret_finish(o_h, rg[:, h * R_DV:(h + 1) * R_DV])
        rows = slice(h * R_DK, (h + 1) * R_DK)
        s_scr[rows, :] = s_old[rows] * cdec_ref[rows, :] + _dot(kdb[rows], vh)
    s_ref[0] = s_scr[...]


def _ret_prompt(z, rkt, rtab, ctab, consts, s0, batch, seq):
    chunk = math.gcd(seq, R_CHUNK)
    nc = seq // chunk
    dmat, qdec, kdec, cdec = consts
    dk = R_HEADS * R_DK
    const = lambda shape: pl.BlockSpec(shape, lambda b, c: (0,) * len(shape))
    return pl.pallas_call(
        _ret_prompt_kernel,
        grid=(batch, nc),
        in_specs=[
            pl.BlockSpec((chunk, W_RETBLK), lambda b, c: (b * nc + c, Z_RET // W_RETBLK)),
            pl.BlockSpec((dk, chunk), lambda b, c: (0, b * nc + c)),
            pl.BlockSpec((3, chunk, LANE), lambda b, c: (0, c, 0)),
            pl.BlockSpec((2, R_DK // 2, chunk), lambda b, c: (0, 0, c)),
            const((R_HEADS, chunk, chunk)), const((chunk, R_HEADS * R_DV)),
            const((dk, chunk)), const((dk, R_DV)),
            pl.BlockSpec((1, dk, R_DV), lambda b, c: (b, 0, 0)),
        ],
        out_specs=[pl.BlockSpec((chunk, R_HEADS * R_DV), lambda b, c: (b * nc + c, 0)),
                   pl.BlockSpec((1, dk, R_DV), lambda b, c: (b, 0, 0))],
        out_shape=[jax.ShapeDtypeStruct((batch * seq, R_HEADS * R_DV), BF16),
                   jax.ShapeDtypeStruct((batch, dk, R_DV), F32)],
        scratch_shapes=[pltpu.VMEM((dk, R_DV), F32)],
        compiler_params=_params(("parallel", "arbitrary")),
        name="ret_prompt",
    )(z, rkt, rtab, ctab, dmat, qdec, kdec, cdec, s0)


def _ret_sample_kernel(zr_ref, rkt_ref, rtab_ref, ctab_ref, dmat_ref, qdec_ref, kdec_ref, cdec_ref,
                       s0_ref, o_ref, s_ref, ocr_scr, *, dec_seq):
    q, kt, v, rg = _ret_prepare(zr_ref, rkt_ref, rtab_ref, ctab_ref)
    n = q.shape[0]
    nseq = n // dec_seq
    ktb = kt.astype(BF16)
    kd = kt * kdec_ref[...]
    vb = v.astype(BF16)
    qms = [jnp.where(_head_lane_mask(h, n), q, 0.0).astype(BF16) for h in range(R_HEADS)]
    tok = lax.broadcasted_iota(jnp.int32, (R_DK, n), 1) // dec_seq

    for b in range(nseq):
        r0 = b * dec_seq
        sb = s0_ref[b].astype(BF16)
        qstack = jnp.concatenate([qm[r0:r0 + dec_seq] for qm in qms], axis=0)
        res = _dot(qstack, sb)
        for h in range(R_HEADS):
            ocr_scr[r0:r0 + dec_seq, h * R_DV:(h + 1) * R_DV] = res[h * dec_seq:(h + 1) * dec_seq]

    for h in range(R_HEADS):
        vh = vb[:, h * R_DV:(h + 1) * R_DV]
        hs = slice(h * R_DV, (h + 1) * R_DV)
        sc = _dot(qms[h], ktb) * dmat_ref[h]
        o_h = _dot(sc.astype(BF16), vh) + ocr_scr[:, hs] * qdec_ref[:, hs]
        o_ref[:, hs] = _ret_finish(o_h, rg[:, hs])
        rows = slice(h * R_DK, (h + 1) * R_DK)
        kdh = kd[rows]
        for b in range(nseq):
            kb = jnp.where(tok == b, kdh, 0.0).astype(BF16)
            s_ref[b, rows, :] = s0_ref[b, rows, :] * cdec_ref[rows, :] + _dot(kb, vh)


def _ret_sample(z, rkt, rtab, ctab, consts, s0, dec_seq):
    t = z.shape[0]
    n = min(LANE, t)
    nseq = n // dec_seq
    dmat, qdec, kdec, cdec = consts
    dk = R_HEADS * R_DK
    const = lambda shape: pl.BlockSpec(shape, lambda s: (0,) * len(shape))
    return pl.pallas_call(
        functools.partial(_ret_sample_kernel, dec_seq=dec_seq),
        grid=(t // n,),
        in_specs=[
            pl.BlockSpec((n, W_RETBLK), lambda s: (s, Z_RET // W_RETBLK)),
            pl.BlockSpec((dk, n), lambda s: (0, s)),
            pl.BlockSpec((3, n, LANE), lambda s: (0, s, 0)),
            pl.BlockSpec((2, R_DK // 2, n), lambda s: (0, 0, s)),
            const((R_HEADS, n, n)), const((n, R_HEADS * R_DV)),
            const((dk, n)), const((dk, R_DV)),
            pl.BlockSpec((nseq, dk, R_DV), lambda s: (s, 0, 0)),
        ],
        out_specs=[pl.BlockSpec((n, R_HEADS * R_DV), lambda s: (s, 0)),
                   pl.BlockSpec((nseq, dk, R_DV), lambda s: (s, 0, 0))],
        out_shape=[jax.ShapeDtypeStruct((t, R_HEADS * R_DV), BF16),
                   jax.ShapeDtypeStruct((t // dec_seq, dk, R_DV), F32)],
        scratch_shapes=[pltpu.VMEM((n, R_HEADS * R_DV), F32)],
        compiler_params=_params(("parallel",)),
        name="ret_sample",
    )(z, rkt, rtab, ctab, dmat, qdec, kdec, cdec, s0)


def _merge_tail(y_conv, gate_ref, omla_ref, oret_ref, x_ref, wco_ref, wmo_ref, wro_ref, wo_ref, y_ref):
    br_a = _dot(y_conv.astype(BF16), wco_ref[...])
    br_b = _dot(omla_ref[...], wmo_ref[...])
    br_c = _dot(oret_ref[...], wro_ref[...])
    mixed = (jax.nn.sigmoid(gate_ref[:, :D_MODEL]) * br_a
             + jax.nn.sigmoid(gate_ref[:, D_MODEL:2 * D_MODEL]) * br_b
             + jax.nn.sigmoid(gate_ref[:, 2 * D_MODEL:]) * br_c)
    y_ref[...] = x_ref[...] + _dot(mixed.astype(BF16), wo_ref[...])


def _merge_prompt_kernel(zc_ref, gate_ref, omla_ref, oret_ref, x_ref, st_ref, wc_ref,
                         wco_ref, wmo_ref, wro_ref, wo_ref, y_ref, nb_ref, ext_scr):
    tm = zc_ref.shape[0]

    @pl.when(pl.program_id(1) == 0)
    def _():
        ext_scr[0:SUBLANE, :] = st_ref[0]

    cb = zc_ref[:, :CONV_DIM]
    u = zc_ref[:, CONV_DIM:2 * CONV_DIM] * zc_ref[:, 2 * CONV_DIM:]
    ext_scr[SUBLANE:SUBLANE + tm, :] = u
    conv = (ext_scr[SUBLANE - 2:SUBLANE - 2 + tm, :] * wc_ref[0:1, :]
            + ext_scr[SUBLANE - 1:SUBLANE - 1 + tm, :] * wc_ref[1:2, :]
            + u * wc_ref[2:3, :])
    tail = ext_scr[tm:tm + SUBLANE, :]
    ext_scr[0:SUBLANE, :] = tail
    nb_ref[0] = tail
    _merge_tail(cb * conv, gate_ref, omla_ref, oret_ref, x_ref, wco_ref, wmo_ref, wro_ref, wo_ref, y_ref)


def _merge_sample_kernel(zc_ref, gate_ref, omla_ref, oret_ref, x_ref, st0_ref, st1_ref, wc_ref,
                         wco_ref, wmo_ref, wro_ref, wo_ref, y_ref, u_ref, *, dec_seq):
    cb = zc_ref[:, :CONV_DIM]
    u = zc_ref[:, CONV_DIM:2 * CONV_DIM] * zc_ref[:, 2 * CONV_DIM:]
    u_ref[...] = u
    pos = lax.broadcasted_iota(jnp.int32, u.shape, 0) % dec_seq
    um1 = jnp.where(pos == 0, st1_ref[...], pltpu.roll(u, 1, 0))
    um2 = jnp.where(pos == 0, st0_ref[...], jnp.where(pos == 1, st1_ref[...], pltpu.roll(u, 2, 0)))
    conv = um2 * wc_ref[0:1, :] + um1 * wc_ref[1:2, :] + u * wc_ref[2:3, :]
    _merge_tail(cb * conv, gate_ref, omla_ref, oret_ref, x_ref, wco_ref, wmo_ref, wro_ref, wo_ref, y_ref)


def _merge_weight_specs(ngrid):
    const = lambda shape: pl.BlockSpec(shape, lambda *_: (0,) * len(shape))
    return [const((SUBLANE, CONV_DIM)), const((CONV_DIM, D_MODEL)), const((HEADS * VDIM, D_MODEL)),
            const((R_HEADS * R_DV, D_MODEL)), const((D_MODEL, D_MODEL))]


def _merge_prompt(z, omla, oret, x, st_pad, weights, batch, seq):
    tm = min(512, seq)
    nt = seq // tm
    row = lambda b, i: b * nt + i
    return pl.pallas_call(
        _merge_prompt_kernel,
        grid=(batch, nt),
        in_specs=[
            pl.BlockSpec((tm, W_CONVBLK), lambda b, i: (row(b, i), Z_CONV // W_CONVBLK)),
            pl.BlockSpec((tm, 3 * D_MODEL), lambda b, i: (row(b, i), Z_GATE // (3 * D_MODEL))),
            pl.BlockSpec((tm, HEADS * VDIM), lambda b, i: (row(b, i), 0)),
            pl.BlockSpec((tm, R_HEADS * R_DV), lambda b, i: (row(b, i), 0)),
            pl.BlockSpec((tm, D_MODEL), lambda b, i: (row(b, i), 0)),
            pl.BlockSpec((1, SUBLANE, CONV_DIM), lambda b, i: (b, 0, 0)),
        ] + _merge_weight_specs(2),
        out_specs=[pl.BlockSpec((tm, D_MODEL), lambda b, i: (row(b, i), 0)),
                   pl.BlockSpec((1, SUBLANE, CONV_DIM), lambda b, i: (b, 0, 0))],
        out_shape=[jax.ShapeDtypeStruct((batch * seq, D_MODEL), F32),
                   jax.ShapeDtypeStruct((batch, SUBLANE, CONV_DIM), F32)],
        scratch_shapes=[pltpu.VMEM((tm + SUBLANE, CONV_DIM), F32)],
        compiler_params=_params(("parallel", "arbitrary")),
        name="merge_prompt",
    )(z, z, omla, oret, x, st_pad, *weights)


def _merge_sample(z, omla, oret, x, st0, st1, weights, dec_seq):
    t = z.shape[0]
    tm = min(512, t)
    return pl.pallas_call(
        functools.partial(_merge_sample_kernel, dec_seq=dec_seq),
        grid=(t // tm,),
        in_specs=[
            pl.BlockSpec((tm, W_CONVBLK), lambda i: (i, Z_CONV // W_CONVBLK)),
            pl.BlockSpec((tm, 3 * D_MODEL), lambda i: (i, Z_GATE // (3 * D_MODEL))),
            pl.BlockSpec((tm, HEADS * VDIM), lambda i: (i, 0)),
            pl.BlockSpec((tm, R_HEADS * R_DV), lambda i: (i, 0)),
            pl.BlockSpec((tm, D_MODEL), lambda i: (i, 0)),
            pl.BlockSpec((tm, CONV_DIM), lambda i: (i, 0)),
            pl.BlockSpec((tm, CONV_DIM), lambda i: (i, 0)),
        ] + _merge_weight_specs(1),
        out_specs=[pl.BlockSpec((tm, D_MODEL), lambda i: (i, 0)),
                   pl.BlockSpec((tm, CONV_DIM), lambda i: (i, 0))],
        out_shape=[jax.ShapeDtypeStruct((t, D_MODEL), F32),
                   jax.ShapeDtypeStruct((t, CONV_DIM), F32)],
        compiler_params=_params(("parallel",)),
        name="merge_sample",
    )(z, z, omla, oret, x, st0, st1, *weights)


def _ffn_kernel(x_ref, g_ref, wu_ref, wd_ref, y_ref, h_scr, acc_scr):
    j = pl.program_id(1)

    @pl.when(j == 0)
    def _():
        x = x_ref[...]
        ms = jnp.mean(x * x, axis=-1, keepdims=True)
        h_scr[...] = (x * lax.rsqrt(ms + EPS) * g_ref[...]).astype(BF16)
        acc_scr[...] = x

    a = jnp.maximum(_dot(h_scr[...], wu_ref[...]), 0.0)
    acc_scr[...] += _dot((a * a).astype(BF16), wd_ref[...])

    @pl.when(j == pl.num_programs(1) - 1)
    def _():
        y_ref[...] = acc_scr[...]


def _ffn(x, g, wu, wd):
    t = x.shape[0]
    tm = min(1024, t)
    tf = 1024
    return pl.pallas_call(
        _ffn_kernel,
        grid=(t // tm, D_FF // tf),
        in_specs=[
            pl.BlockSpec((tm, D_MODEL), lambda i, j: (i, 0)),
            pl.BlockSpec((1, D_MODEL), lambda i, j: (0, 0)),
            pl.BlockSpec((D_MODEL, tf), lambda i, j: (0, j)),
            pl.BlockSpec((tf, D_MODEL), lambda i, j: (j, 0)),
        ],
        out_specs=pl.BlockSpec((tm, D_MODEL), lambda i, j: (i, 0)),
        out_shape=jax.ShapeDtypeStruct((t, D_MODEL), F32),
        scratch_shapes=[pltpu.VMEM((tm, D_MODEL), BF16), pltpu.VMEM((tm, D_MODEL), F32)],
        compiler_params=_params(("parallel", "arbitrary")),
        name="ffn",
    )(x, g, wu, wd)


def _rope_tables(pos):
    posf = pos.astype(F32)[:, None]
    n = pos.shape[0]

    def cs(half):
        inv = THETA ** (-jnp.arange(half, dtype=F32) / half)
        ang = posf * inv[None, :]
        return jnp.cos(ang), jnp.sin(ang)

    c16, s16 = cs(ROPE // 2)
    one = lambda w: jnp.ones((n, w), F32)
    zero = lambda w: jnp.zeros((n, w), F32)
    qtab = jnp.stack([
        jnp.concatenate([one(NOPE), c16, c16, one(HT - QK)], axis=1),
        jnp.concatenate([zero(NOPE), -s16, zero(HT - NOPE - ROPE // 2)], axis=1),
        jnp.concatenate([zero(NOPE + ROPE // 2), s16, zero(HT - QK)], axis=1)])
    ktab = jnp.stack([
        jnp.concatenate([c16, c16, one(LANE - ROPE)], axis=1),
        jnp.concatenate([-s16, zero(LANE - ROPE // 2)], axis=1),
        jnp.concatenate([zero(ROPE // 2), s16, zero(LANE - ROPE)], axis=1)])
    c32, s32 = cs(R_DK // 2)
    z32 = zero(R_DK // 2)
    rtab = jnp.stack([
        jnp.concatenate([c32] * (LANE // (R_DK // 2)), axis=1),
        jnp.concatenate([-s32, z32] * (LANE // R_DK), axis=1),
        jnp.concatenate([z32, s32] * (LANE // R_DK), axis=1)])
    ctab = jnp.stack([c32.T, s32.T])
    return qtab, ktab, rtab, ctab


def _ret_consts(n, chunk):
    log_g = jnp.log1p(-(2.0 ** (-5.0 - jnp.arange(R_HEADS, dtype=F32))))
    idx = jnp.arange(n)
    loc = (idx % chunk).astype(F32)
    same = (idx[:, None] // chunk) == (idx[None, :] // chunk)
    diff = loc[:, None] - loc[None, :]
    dmat = jnp.where(same & (diff >= 0), jnp.exp(log_g[:, None, None] * jnp.maximum(diff, 0.0)), 0.0)
    qd = jnp.exp(log_g[None, :] * (loc[:, None] + 1.0))
    qdec = jnp.repeat(qd, R_DV, axis=1)
    kd = jnp.exp(log_g[:, None] * (chunk - 1.0 - loc[None, :]))
    kdec = jnp.repeat(kd, R_DK, axis=0)
    cdec = jnp.repeat(jnp.exp(log_g * chunk)[:, None], R_DK, axis=0) * jnp.ones((1, R_DV), F32)
    return dmat.astype(F32), qdec, kdec, cdec


def _layer_weights(l, g_mix, w_in, w_conv, w_conv_out, g_q_lat, w_uq, g_kv_lat, w_ukv, g_qn, g_kn,
                   w_mla_out, w_ret_out, w_o, g_ffn, w_up, w_down):
    wi = w_in[l]
    o_cq, o_rq, o_rk, o_gl, o_end = 1536, 2208, 2464, 3744, 6816
    w_perm = jnp.concatenate(
        [wi[:, :o_cq], wi[:, o_rq:o_gl], wi[:, o_gl:o_end], wi[:, o_cq:o_rq],
         jnp.zeros((D_MODEL, LANE - ROPE), F32)], axis=1).astype(BF16)
    w_rk_t = wi[:, o_rk:o_rk + R_HEADS * R_DK].T.astype(BF16)
    wuq = jnp.pad(w_uq[l].reshape(Q_LORA, HEADS, QK), ((0, 0), (0, 0), (0, HT - QK)))
    wuq = wuq.reshape(Q_LORA, HEADS * HT).astype(BF16)
    wkv = w_ukv[l].reshape(KV_LORA, HEADS, NOPE + VDIM)
    w_uk, w_uv = wkv[..., :NOPE], wkv[..., NOPE:]
    wk_pad = jnp.pad(w_uk, ((0, 0), (0, 0), (0, HT - NOPE))).reshape(KV_LORA, HEADS * HT).astype(BF16)
    wv = w_uv.reshape(KV_LORA, HEADS * VDIM).astype(BF16)
    wv_pad = jnp.pad(w_uv, ((0, 0), (0, 0), (0, HT - VDIM))).reshape(KV_LORA, HEADS * HT).T.astype(BF16)
    wukt = w_uk.reshape(KV_LORA, HEADS * NOPE).T.astype(BF16)
    j = np.arange(ROPE)
    pm = np.zeros((LANE, HEADS * HT), np.float32)
    for h in range(HEADS):
        pm[j, h * HT + NOPE + j] = 1.0
    shift = np.zeros((HT, LANE), np.float32)
    shift[NOPE + j, j] = 1.0
    wabs = jnp.pad(jnp.transpose(w_uk, (1, 2, 0)), ((0, 0), (0, HT - NOPE), (0, 0)))
    wcomb = jnp.concatenate([wabs, jnp.broadcast_to(jnp.asarray(shift), (HEADS, HT, LANE))],
                            axis=2).astype(BF16)
    pad_gain = lambda g: jnp.pad(g, (0, HT - QK))[None, :]
    return dict(
        g_mix=g_mix[l][None, :], w_perm=w_perm, w_rk_t=w_rk_t,
        g_q_lat=g_q_lat[l][None, :], wuq=wuq, g_kv_lat=g_kv_lat[l][None, :],
        gqn=pad_gain(g_qn[l]) * (QK ** -0.5), gkn=pad_gain(g_kn[l]),
        wk_pad=wk_pad, pmat=jnp.asarray(pm, BF16), wv=wv, wv_pad=wv_pad, wukt=wukt, wcomb=wcomb,
        merge=(jnp.pad(w_conv[l], ((0, SUBLANE - CONV_W), (0, 0))), w_conv_out[l].astype(BF16),
               w_mla_out[l].astype(BF16), w_ret_out[l].astype(BF16), w_o[l].astype(BF16)),
        g_ffn=g_ffn[l][None, :], w_up=w_up[l].astype(BF16), w_down=w_down[l].astype(BF16))


def kernel(x_prompt, x_sample, cache_ckv, cache_kpe, state_conv, state_ret, page_table, g_mix, w_in,
           w_conv, w_conv_out, g_q_lat, w_uq, g_kv_lat, w_ukv, g_qn, g_kn, w_mla_out, w_ret_out, w_o,
           g_ffn, w_up, w_down):
    batch, seq, _ = x_prompt.shape
    nb, dec_seq, _ = x_sample.shape
    depth = w_in.shape[0]
    past = page_table.shape[1] * PAGE
    assert dec_seq == SUBLANE
    tp, ts = batch * seq, nb * dec_seq
    dk = R_HEADS * R_DK

    tabs_p = _rope_tables(jnp.arange(seq))
    tabs_s = _rope_tables(jnp.tile(past + jnp.arange(dec_seq), nb))
    chunk_p = math.gcd(seq, R_CHUNK)
    consts_p = _ret_consts(chunk_p, chunk_p)
    consts_s = _ret_consts(min(LANE, ts), dec_seq)
    tm_p, tm_s = min(512, seq), min(512, ts)

    cache_kpe_t = jnp.swapaxes(cache_kpe, 2, 3)
    yp = x_prompt.reshape(tp, D_MODEL)
    ys = x_sample.reshape(ts, D_MODEL)
    conv0 = jnp.zeros((batch, SUBLANE, CONV_DIM), F32)
    ret0 = jnp.zeros((batch, dk, R_DV), F32)
    outs = {k: [] for k in ("ckv_p", "kpe_p", "conv_p", "ret_p", "ckv_s", "kpe_s", "conv_s", "ret_s")}

    for l in range(depth):
        w = _layer_weights(l, g_mix, w_in, w_conv, w_conv_out, g_q_lat, w_uq, g_kv_lat, w_ukv, g_qn,
                           g_kn, w_mla_out, w_ret_out, w_o, g_ffn, w_up, w_down)

        qtab, ktab, rtab, ctab = tabs_p
        z, rkt = _inproj(yp, w["g_mix"], w["w_perm"], w["w_rk_t"])
        q, ckvn, kper, k, v = _mlaprep(z, qtab, ktab, tm_p, w["g_q_lat"], w["wuq"], w["g_kv_lat"],
                                       w["gqn"], (w["wk_pad"], w["pmat"], w["wv_pad"], w["gkn"]))
        omla = _flash(q, k, v, batch, seq)
        oret, snew = _ret_prompt(z, rkt, rtab, ctab, consts_p, ret0, batch, seq)
        ymid, nbuf = _merge_prompt(z, omla, oret, yp, conv0, w["merge"], batch, seq)
        yp = _ffn(ymid, w["g_ffn"], w["w_up"], w["w_down"])
        outs["ckv_p"].append(ckvn.reshape(batch, seq, KV_LORA))
        outs["kpe_p"].append(kper.reshape(batch, seq, ROPE))
        outs["conv_p"].append(nbuf[:, SUBLANE - (CONV_W - 1):, :])
        outs["ret_p"].append(snew.reshape(batch, R_HEADS, R_DK, R_DV))

        qtab, ktab, rtab, ctab = tabs_s
        z, rkt = _inproj(ys, w["g_mix"], w["w_perm"], w["w_rk_t"])
        q, ckvn, kper = _mlaprep(z, qtab, ktab, tm_s, w["g_q_lat"], w["wuq"], w["g_kv_lat"],
                                 w["gqn"], None)
        qa, qp = _qabs(q, w["gkn"], w["wcomb"], nb, dec_seq)
        cnew = jnp.pad(ckvn.reshape(nb, dec_seq, KV_LORA), ((0, 0), (0, PAGE - dec_seq), (0, 0)))
        pnew_t = jnp.pad(jnp.swapaxes(kper.reshape(nb, dec_seq, ROPE), 1, 2),
                         ((0, 0), (0, 0), (0, PAGE - dec_seq)))
        omla = _decode(page_table, l, cache_ckv, cache_kpe_t,
                       qa.reshape(nb, HEADS * dec_seq, KV_LORA), qp.reshape(nb, HEADS * dec_seq, LANE),
                       w["wukt"], w["wv"], cnew, pnew_t).reshape(ts, HEADS * VDIM)
        oret, snew = _ret_sample(z, rkt, rtab, ctab, consts_s, state_ret[l].reshape(nb, dk, R_DV), dec_seq)
        st = state_conv[l]
        st0 = jnp.repeat(st[:, 0, :], dec_seq, axis=0)
        st1 = jnp.repeat(st[:, 1, :], dec_seq, axis=0)
        ymid, u = _merge_sample(z, omla, oret, ys, st0, st1, w["merge"], dec_seq)
        ys = _ffn(ymid, w["g_ffn"], w["w_up"], w["w_down"])
        outs["ckv_s"].append(ckvn.reshape(nb, dec_seq, KV_LORA))
        outs["kpe_s"].append(kper.reshape(nb, dec_seq, ROPE))
        outs["conv_s"].append(u.reshape(nb, dec_seq, CONV_DIM)[:, dec_seq - (CONV_W - 1):, :])
        outs["ret_s"].append(snew.reshape(nb, R_HEADS, R_DK, R_DV))

    return (yp.reshape(batch, seq, D_MODEL), ys.reshape(nb, dec_seq, D_MODEL),
            jnp.stack(outs["ckv_p"]), jnp.stack(outs["kpe_p"]), jnp.stack(outs["conv_p"]),
            jnp.stack(outs["ret_p"]),
            jnp.stack(outs["ckv_s"]), jnp.stack(outs["kpe_s"]), jnp.stack(outs["conv_s"]),
            jnp.stack(outs["ret_s"]))
```

```python
import functools
import math

import jax
import jax.numpy as jnp
import numpy as np
from jax import lax
from jax.experimental import pallas as pl
from jax.experimental.pallas import tpu as pltpu

F32 = jnp.float32
BF16 = jnp.bfloat16

D_MODEL = 1024
PAGE = 128
CONV_DIM = 512
CONV_W = 3
HEADS = 8
NOPE = 64
ROPE = 32
QK = NOPE + ROPE
VDIM = 64
Q_LORA = 384
KV_LORA = 256
R_HEADS = 4
R_DK = 64
R_DV = 128
R_CHUNK = 128
D_FF = 4096
THETA = 10000.0
EPS = 1e-6
NEG = -1e30

LANE = 128
SUBLANE = 8
HT = LANE

Z_CONV = 0
Z_RET = 1536
Z_GATE = 3072
Z_MLA = 6144
Z_W = 6912
W_CONVBLK = 3 * CONV_DIM
W_RETBLK = 2 * R_HEADS * R_DK + 2 * R_HEADS * R_DV
W_MLABLK = Q_LORA + KV_LORA + LANE

VMEM_LIMIT = 56 * 1024 * 1024


def _dot(a, b):
    return jnp.dot(a, b, preferred_element_type=F32)


def _dot_nt(a, b):
    return lax.dot_general(a, b, (((1,), (1,)), ((), ())), preferred_element_type=F32)


def _params(sem):
    return pltpu.CompilerParams(dimension_semantics=sem, vmem_limit_bytes=VMEM_LIMIT)


def _rope_rows(x, tab_ref, half):
    return (x * tab_ref[0] + pltpu.roll(x, LANE - half, 1) * tab_ref[1]
            + pltpu.roll(x, half, 1) * tab_ref[2])


def _inproj_kernel(x_ref, g_ref, w_ref, wrk_ref, z_ref, rkt_ref, h_scr):
    @pl.when(pl.program_id(1) == 0)
    def _():
        x = x_ref[...]
        ms = jnp.mean(x * x, axis=-1, keepdims=True)
        hb = (x * lax.rsqrt(ms + EPS) * g_ref[...]).astype(BF16)
        h_scr[...] = hb
        rkt_ref[...] = _dot_nt(wrk_ref[...], hb)

    z_ref[...] = _dot(h_scr[...], w_ref[...])


def _inproj(x, g, w_perm, w_rk_t):
    t = x.shape[0]
    tm = min(1024, t)
    tn = 2304
    return pl.pallas_call(
        _inproj_kernel,
        grid=(t // tm, Z_W // tn),
        in_specs=[
            pl.BlockSpec((tm, D_MODEL), lambda i, j: (i, 0)),
            pl.BlockSpec((1, D_MODEL), lambda i, j: (0, 0)),
            pl.BlockSpec((D_MODEL, tn), lambda i, j: (0, j)),
            pl.BlockSpec((R_HEADS * R_DK, D_MODEL), lambda i, j: (0, 0)),
        ],
        out_specs=[
            pl.BlockSpec((tm, tn), lambda i, j: (i, j)),
            pl.BlockSpec((R_HEADS * R_DK, tm), lambda i, j: (0, i)),
        ],
        out_shape=[jax.ShapeDtypeStruct((t, Z_W), F32),
                   jax.ShapeDtypeStruct((R_HEADS * R_DK, t), F32)],
        scratch_shapes=[pltpu.VMEM((tm, D_MODEL), BF16)],
        compiler_params=_params(("parallel", "arbitrary")),
        name="inproj",
    )(x, g, w_perm, w_rk_t)


def _mlaprep_kernel(z_ref, qtab_ref, ktab_ref, gq_ref, wuq_ref, gkv_ref, gqn_ref,
                    *rest, expand):
    if expand:
        wk_ref, pmat_ref, wv_ref, gkn_ref, q_ref, ckv_ref, kpe_ref, k_ref, v_ref = rest
    else:
        q_ref, ckv_ref, kpe_ref = rest
    cq = z_ref[:, :Q_LORA]
    ckv = z_ref[:, Q_LORA:Q_LORA + KV_LORA]
    kpe = z_ref[:, Q_LORA + KV_LORA:]

    cqn = cq * lax.rsqrt(jnp.mean(cq * cq, axis=-1, keepdims=True) + EPS) * gq_ref[...]
    qf = _dot(cqn.astype(BF16), wuq_ref[...])
    for h in range(HEADS):
        xr = _rope_rows(qf[:, h * HT:(h + 1) * HT], qtab_ref, ROPE // 2)
        ms = jnp.sum(xr * xr, axis=-1, keepdims=True) * (1.0 / QK)
        q_ref[:, h * HT:(h + 1) * HT] = (xr * lax.rsqrt(ms + EPS) * gqn_ref[...]).astype(BF16)

    ckvn = ckv * lax.rsqrt(jnp.mean(ckv * ckv, axis=-1, keepdims=True) + EPS) * gkv_ref[...]
    ckv_ref[...] = ckvn
    kper = _rope_rows(kpe, ktab_ref, ROPE // 2)
    kpe_ref[...] = kper[:, :ROPE]

    if expand:
        cb = ckvn.astype(BF16)
        p_hi = kper.astype(BF16)
        p_lo = (kper - p_hi.astype(F32)).astype(BF16)
        kf = _dot(cb, wk_ref[...]) + _dot(p_hi, pmat_ref[...]) + _dot(p_lo, pmat_ref[...])
        for h in range(HEADS):
            xk = kf[:, h * HT:(h + 1) * HT]
            ms = jnp.sum(xk * xk, axis=-1, keepdims=True) * (1.0 / QK)
            k_ref[:, h * HT:(h + 1) * HT] = (xk * lax.rsqrt(ms + EPS) * gkn_ref[...]).astype(BF16)
        vt = _dot_nt(wv_ref[...], cb)
        row = lax.broadcasted_iota(jnp.int32, vt.shape, 0) % HT
        v_ref[...] = jnp.where(row == VDIM, 1.0, vt).astype(BF16)


def _mlaprep(z, qtab, ktab, tm, gq, wuq, gkv, gqn, expand_args):
    t = z.shape[0]
    n_seq_tiles = qtab.shape[1] // tm
    expand = expand_args is not None
    const = lambda shape: pl.BlockSpec(shape, lambda i: (0,) * len(shape))
    tab = pl.BlockSpec((3, tm, LANE), lambda i: (0, i % n_seq_tiles, 0))
    in_specs = [
        pl.BlockSpec((tm, W_MLABLK), lambda i: (i, Z_MLA // W_MLABLK)),
        tab, tab,
        const((1, Q_LORA)), const((Q_LORA, HEADS * HT)), const((1, KV_LORA)), const((1, HT)),
    ]
    args = [z, qtab, ktab, gq, wuq, gkv, gqn]
    out_specs = [pl.BlockSpec((tm, HEADS * HT), lambda i: (i, 0)),
                 pl.BlockSpec((tm, KV_LORA), lambda i: (i, 0)),
                 pl.BlockSpec((tm, ROPE), lambda i: (i, 0))]
    out_shape = [jax.ShapeDtypeStruct((t, HEADS * HT), BF16),
                 jax.ShapeDtypeStruct((t, KV_LORA), F32),
                 jax.ShapeDtypeStruct((t, ROPE), F32)]
    if expand:
        wk, pmat, wv, gkn = expand_args
        in_specs += [const((KV_LORA, HEADS * HT)), const((LANE, HEADS * HT)),
                     const((HEADS * HT, KV_LORA)), const((1, HT))]
        args += [wk, pmat, wv, gkn]
        out_specs += [pl.BlockSpec((tm, HEADS * HT), lambda i: (i, 0)),
                      pl.BlockSpec((HEADS * HT, tm), lambda i: (0, i))]
        out_shape += [jax.ShapeDtypeStruct((t, HEADS * HT), BF16),
                      jax.ShapeDtypeStruct((HEADS * HT, t), BF16)]
    return pl.pallas_call(
        functools.partial(_mlaprep_kernel, expand=expand),
        grid=(t // tm,),
        in_specs=in_specs, out_specs=out_specs, out_shape=out_shape,
        compiler_params=_params(("parallel",)),
        name="mlaprep_expand" if expand else "mlaprep",
    )(*args)


def _flash_kernel(q_ref, k_ref, vt_ref, o_ref, sa_scr, sb_scr, m_scr, acc_scr, *, tq, tk):
    qi = pl.program_id(2)
    m_scr[...] = jnp.full(m_scr.shape, NEG, F32)
    acc_scr[...] = jnp.zeros(acc_scr.shape, F32)

    def scores(ki, s_scr):
        start = pl.multiple_of(ki * tk, tk)
        for h in range(2):
            hs = slice(h * HT, (h + 1) * HT)
            s_scr[h] = _dot_nt(k_ref[pl.ds(start, tk), hs], q_ref[:, hs])

    def attend(ki, s_scr, diag):
        start = pl.multiple_of(ki * tk, tk)
        for h in range(2):
            s = s_scr[h]
            if diag is not None:
                key = lax.broadcasted_iota(jnp.int32, (tk, tq), 0) + diag * tk
                qry = lax.broadcasted_iota(jnp.int32, (tk, tq), 1)
                s = jnp.where(key <= qry, s, NEG)
            m = m_scr[h]
            m_new = jnp.maximum(m, jnp.max(s, axis=0, keepdims=True))
            p = jnp.exp2(s - m_new).astype(BF16)
            vt = vt_ref[h * HT:(h + 1) * HT, pl.ds(start, tk)]
            acc_scr[h] = jnp.exp2(m - m_new) * acc_scr[h] + _dot(vt, p)
            m_scr[h] = m_new

    def body(j, carry):
        scores(2 * j + 1, sb_scr)
        attend(2 * j, sa_scr, None)
        scores(2 * j + 2, sa_scr)
        attend(2 * j + 1, sb_scr, None)
        return carry

    scores(0, sa_scr)
    lax.fori_loop(0, qi, body, 0)
    scores(2 * qi + 1, sb_scr)
    attend(2 * qi, sa_scr, 0)
    attend(2 * qi + 1, sb_scr, 1)
    outs = []
    for h in range(2):
        a = acc_scr[h]
        outs.append((a / a[VDIM:VDIM + 1, :]).T)
    lane = lax.broadcasted_iota(jnp.int32, (tq, LANE), 1)
    o_ref[...] = jnp.where(lane < VDIM, outs[0], pltpu.roll(outs[1], VDIM, 1)).astype(BF16)


def _flash(q, k, vt, batch, seq):
    tq = min(512, seq)
    tk = tq // 2
    nq = seq // tq
    return pl.pallas_call(
        functools.partial(_flash_kernel, tq=tq, tk=tk),
        grid=(batch, HEADS // 2, nq),
        in_specs=[
            pl.BlockSpec((tq, 2 * HT), lambda b, hp, i: (b * nq + i, hp)),
            pl.BlockSpec((seq, 2 * HT), lambda b, hp, i: (b, hp)),
            pl.BlockSpec((2 * HT, seq), lambda b, hp, i: (hp, b)),
        ],
        out_specs=pl.BlockSpec((tq, 2 * VDIM), lambda b, hp, i: (b * nq + i, hp)),
        out_shape=jax.ShapeDtypeStruct((batch * seq, HEADS * VDIM), BF16),
        scratch_shapes=[pltpu.VMEM((2, tk, tq), F32), pltpu.VMEM((2, tk, tq), F32),
                        pltpu.VMEM((2, 1, tq), F32), pltpu.VMEM((2, HT, tq), F32)],
        compiler_params=_params(("parallel", "parallel", "arbitrary")),
        name="flash",
    )(q, k, vt)


def _qabs_kernel(q_ref, gkn_ref, wcomb_ref, qa_ref, qp_ref):
    nb = qa_ref.shape[0]
    for h in range(HEADS):
        qg = (q_ref[:, h * HT:(h + 1) * HT].astype(F32) * gkn_ref[...]).astype(BF16)
        r = _dot(qg, wcomb_ref[h])
        qa_ref[:, h] = r[:, :KV_LORA].reshape(nb, SUBLANE, KV_LORA)
        qp_ref[:, h] = r[:, KV_LORA:].reshape(nb, SUBLANE, LANE)


def _qabs(q, gkn, wcomb, nb, dec_seq):
    t = q.shape[0]
    return pl.pallas_call(
        _qabs_kernel,
        grid=(1,),
        in_specs=[pl.BlockSpec((t, HEADS * HT), lambda i: (0, 0)),
                  pl.BlockSpec((1, HT), lambda i: (0, 0)),
                  pl.BlockSpec((HEADS, HT, KV_LORA + LANE), lambda i: (0, 0, 0))],
        out_specs=[pl.BlockSpec((nb, HEADS, dec_seq, KV_LORA), lambda i: (0, 0, 0, 0)),
                   pl.BlockSpec((nb, HEADS, dec_seq, LANE), lambda i: (0, 0, 0, 0))],
        out_shape=[jax.ShapeDtypeStruct((nb, HEADS, dec_seq, KV_LORA), F32),
                   jax.ShapeDtypeStruct((nb, HEADS, dec_seq, LANE), F32)],
        compiler_params=_params(("arbitrary",)),
        name="qabs",
    )(q, gkn, wcomb)


DEC_ROWS = HEADS * SUBLANE
DEC_KROWS = 3 * ROPE + ROPE
DEC_PROWS = DEC_ROWS + 16


def _decode_kernel(pt_ref, qa_ref, qp_ref, wukt_ref, wv_ref, cnew_ref, pnew_ref, ckv_hbm, kpe_hbm,
                   o_ref, cbuf, pbuf, sem, cbf_a, cbf_b, kt_a, kt_b, s_a, s_b, wall, lpe,
                   m_scr, l_scr, acc_scr, *, npg, sub, grp, layer):
    b = pl.program_id(0)
    g = pl.program_id(1)
    ng = pl.num_programs(1)
    step = b * ng + g
    slot = step % 2
    nk = HEADS * NOPE
    half = npg // 2
    width = half * PAGE
    buf_a = (cbf_a, kt_a, s_a)
    buf_b = (cbf_b, kt_b, s_b)

    def page_copies(bb, gg, sl):
        copies = []
        for i in range(npg):
            pid = 0 if bb is None else pt_ref[bb, gg * npg + i]
            copies.append(pltpu.make_async_copy(ckv_hbm.at[layer, pid], cbuf.at[sl, i], sem.at[0, sl]))
            copies.append(pltpu.make_async_copy(kpe_hbm.at[layer, pid], pbuf.at[sl, i], sem.at[1, sl]))
        return copies

    @pl.when(step == 0)
    def _():
        for cp in page_copies(b, g, slot):
            cp.start()
        cbf_b[...] = jnp.zeros(cbf_b.shape, BF16)
        s_b[...] = jnp.zeros(s_b.shape, F32)
        m_scr[...] = jnp.zeros(m_scr.shape, F32)
        l_scr[...] = jnp.zeros(l_scr.shape, F32)
        acc_scr[...] = jnp.zeros(acc_scr.shape, F32)
        for kt in (kt_a, kt_b):
            kt[3 * ROPE:, :] = jnp.zeros((DEC_KROWS - 3 * ROPE, kt.shape[1]), BF16)

    wrap = g + 1 == ng
    b_next = jnp.where(wrap, jnp.where(b + 1 == pl.num_programs(0), 0, b + 1), b)
    prefetch = page_copies(b_next, jnp.where(wrap, 0, g + 1), 1 - slot)
    for cp in prefetch:
        cp.start()

    for cp in page_copies(None, None, slot):
        cp.wait()

    wall[:nk, :] = wukt_ref[...]
    wall[nk:, :] = qa_ref[0].astype(BF16)
    lane = lax.broadcasted_iota(jnp.int32, (DEC_PROWS - DEC_ROWS, DEC_KROWS), 1)
    lpe[:DEC_ROWS, :] = qp_ref[0].astype(BF16)
    lpe[DEC_ROWS:, :] = jnp.where((lane >= ROPE) & (lane < 3 * ROPE), 1.0, 0.0).astype(BF16)

    def stage(bufs, i, c, kp_t):
        cbf, kt, _ = bufs
        cols = slice(i * PAGE, (i + 1) * PAGE)
        cbf[cols, :] = c.astype(BF16)
        kt[:ROPE, cols] = kp_t.astype(BF16)
        kp2 = kp_t * kp_t
        hi = kp2.astype(BF16)
        kt[ROPE:2 * ROPE, cols] = hi
        kt[2 * ROPE:3 * ROPE, cols] = (kp2 - hi.astype(F32)).astype(BF16)

    def scores(bufs, start, w):
        cbf, kt, s_scr = bufs
        ct = cbf[start:start + w, :]
        big = _dot_nt(wall[...], ct)
        ext = _dot(lpe[...], kt[:, start:start + w])
        kpsq = ext[DEC_ROWS:DEC_ROWS + 1]
        for h in range(HEADS):
            blk = big[h * NOPE:(h + 1) * NOPE]
            nsq = jnp.sum(blk * blk, axis=0, keepdims=True)
            rs = lax.rsqrt((nsq + kpsq) * (1.0 / QK) + EPS)
            hr = slice(h * SUBLANE, (h + 1) * SUBLANE)
            s_scr[hr, start:start + w] = (big[nk + h * SUBLANE:nk + (h + 1) * SUBLANE] + ext[hr]) * rs

    def update(bufs, w, mask=None, discard=None):
        cbf, _, s_scr = bufs
        s = s_scr[:, :w]
        if mask is not None:
            s = jnp.where(mask, s, NEG)
        m = m_scr[...]
        m_new = jnp.maximum(m, jnp.max(s, axis=-1, keepdims=True))
        alpha = jnp.exp2(m - m_new)
        acc = alpha * acc_scr[...]
        lsum = alpha * l_scr[...]
        chunk = min(grp, w)
        for c in range(w // chunk):
            cs = slice(c * chunk, (c + 1) * chunk)
            p = jnp.exp2(s_scr[:, cs] - m_new) if mask is None else jnp.exp2(s - m_new)
            lsum = lsum + jnp.sum(p, axis=-1, keepdims=True)
            acc = acc + _dot(p.astype(BF16), cbf[cs, :])
        if discard is not None:
            m_new = jnp.where(discard, NEG, m_new)
            lsum = jnp.where(discard, 0.0, lsum)
            acc = jnp.where(discard, 0.0, acc)
        l_scr[...] = lsum
        acc_scr[...] = acc
        m_scr[...] = m_new

    def half_scores(bufs, first_page):
        pages_per_sub = sub // PAGE
        for j in range(half // pages_per_sub):
            for i in range(j * pages_per_sub, (j + 1) * pages_per_sub):
                stage(bufs, i, cbuf[slot, first_page + i], pbuf[slot, first_page + i])
            scores(bufs, j * sub, sub)

    half_scores(buf_a, 0)
    update(buf_b, width, discard=g == 0)
    half_scores(buf_b, half)
    update(buf_a, width)

    @pl.when(g == ng - 1)
    def _():
        update(buf_b, width)
        stage(buf_a, 0, cnew_ref[0], pnew_ref[0])
        scores(buf_a, 0, PAGE)
        r = lax.broadcasted_iota(jnp.int32, (DEC_ROWS, PAGE), 0)
        t = lax.broadcasted_iota(jnp.int32, (DEC_ROWS, PAGE), 1)
        update(buf_a, PAGE, mask=t <= (r % SUBLANE))
        olat = (acc_scr[...] / l_scr[...]).astype(BF16)
        full = _dot(olat, wv_ref[...])
        colh = lax.broadcasted_iota(jnp.int32, (SUBLANE, HEADS * VDIM), 1) // VDIM
        out = jnp.zeros((SUBLANE, HEADS * VDIM), F32)
        for h in range(HEADS):
            out = out + jnp.where(colh == h, full[h * SUBLANE:(h + 1) * SUBLANE], 0.0)
        o_ref[0] = out.astype(BF16)

    @pl.when(step == pl.num_programs(0) * ng - 1)
    def _():
        for cp in prefetch:
            cp.wait()


def _decode(page_table, layer, cache_ckv, cache_kpe_t, qa, qp, wukt, wv, cnew, pnew_t):
    nb, n_pages = page_table.shape
    npg = min(32, n_pages)
    half_tokens = npg // 2 * PAGE
    sub = min(256, half_tokens)
    ngrp = n_pages // npg

    in_specs = [
        pl.BlockSpec((1, DEC_ROWS, KV_LORA), lambda b, g, pt: (b, 0, 0)),
        pl.BlockSpec((1, DEC_ROWS, DEC_KROWS), lambda b, g, pt: (b, 0, 0)),
        pl.BlockSpec((HEADS * NOPE, KV_LORA), lambda b, g, pt: (0, 0)),
        pl.BlockSpec((KV_LORA, HEADS * VDIM), lambda b, g, pt: (0, 0)),
        pl.BlockSpec((1, PAGE, KV_LORA), lambda b, g, pt: (b, 0, 0)),
        pl.BlockSpec((1, ROPE, PAGE), lambda b, g, pt: (b, 0, 0)),
        pl.BlockSpec(memory_space=pl.ANY),
        pl.BlockSpec(memory_space=pl.ANY),
    ]
    grid_spec = pltpu.PrefetchScalarGridSpec(
        num_scalar_prefetch=1,
        grid=(nb, ngrp),
        in_specs=in_specs,
        out_specs=pl.BlockSpec((1, SUBLANE, HEADS * VDIM), lambda b, g, pt: (b, 0, 0)),
        scratch_shapes=[
            pltpu.VMEM((2, npg, PAGE, KV_LORA), F32),
            pltpu.VMEM((2, npg, ROPE, PAGE), F32),
            pltpu.SemaphoreType.DMA((2, 2)),
            pltpu.VMEM((half_tokens, KV_LORA), BF16),
            pltpu.VMEM((half_tokens, KV_LORA), BF16),
            pltpu.VMEM((DEC_KROWS, half_tokens), BF16),
            pltpu.VMEM((DEC_KROWS, half_tokens), BF16),
            pltpu.VMEM((DEC_ROWS, half_tokens), F32),
            pltpu.VMEM((DEC_ROWS, half_tokens), F32),
            pltpu.VMEM((HEADS * NOPE + DEC_ROWS, KV_LORA), BF16),
            pltpu.VMEM((DEC_PROWS, DEC_KROWS), BF16),
            pltpu.VMEM((DEC_ROWS, 1), F32),
            pltpu.VMEM((DEC_ROWS, 1), F32),
            pltpu.VMEM((DEC_ROWS, KV_LORA), F32),
        ],
    )
    return pl.pallas_call(
        functools.partial(_decode_kernel, npg=npg, sub=sub, grp=min(512, half_tokens), layer=layer),
        grid_spec=grid_spec,
        out_shape=jax.ShapeDtypeStruct((nb, SUBLANE, HEADS * VDIM), BF16),
        compiler_params=_params(("arbitrary", "arbitrary")),
        name="decode",
    )(page_table, qa, qp, wukt, wv, cnew, pnew_t, cache_ckv, cache_kpe_t)


def _ret_prepare(zr_ref, rkt_ref, rtab_ref, ctab_ref):
    dk = R_HEADS * R_DK
    q = jnp.concatenate(
        [_rope_rows(zr_ref[:, i * LANE:(i + 1) * LANE], rtab_ref, R_DK // 2) for i in range(dk // LANE)],
        axis=1)
    cos_t = ctab_ref[0]
    sin_t = ctab_ref[1]
    half = R_DK // 2
    kparts = []
    for h in range(R_HEADS):
        a = rkt_ref[h * R_DK:h * R_DK + half, :]
        b = rkt_ref[h * R_DK + half:(h + 1) * R_DK, :]
        kparts += [a * cos_t - b * sin_t, b * cos_t + a * sin_t]
    kt = jnp.concatenate(kparts, axis=0) * (R_DK ** -0.5)
    v = zr_ref[:, 2 * dk:2 * dk + R_HEADS * R_DV]
    rg = zr_ref[:, 2 * dk + R_HEADS * R_DV:]
    return q, kt, v, rg


def _ret_finish(o_h, rg_h):
    ms = jnp.mean(o_h * o_h, axis=-1, keepdims=True)
    return (jax.nn.silu(rg_h) * (o_h * lax.rsqrt(ms + EPS))).astype(BF16)


def _head_lane_mask(h, n):
    lane = lax.broadcasted_iota(jnp.int32, (n, R_HEADS * R_DK), 1)
    return (lane // R_DK) == h


def _ret_prompt_kernel(zr_ref, rkt_ref, rtab_ref, ctab_ref, dmat_ref, qdec_ref, kdec_ref, cdec_ref,
                       s0_ref, o_ref, s_ref, s_scr):
    c = pl.program_id(1)

    @pl.when(c == 0)
    def _():
        s_scr[...] = s0_ref[0]

    q, kt, v, rg = _ret_prepare(zr_ref, rkt_ref, rtab_ref, ctab_ref)
    chunk = dmat_ref.shape[1]
    nchunk = q.shape[0] // chunk
    vb = v.astype(BF16)
    qms = [jnp.where(_head_lane_mask(h, q.shape[0]), q, 0.0).astype(BF16) for h in range(R_HEADS)]
    states = [s_scr[...]]
    o_in = []
    for c in range(nchunk):
        tok = slice(c * chunk, (c + 1) * chunk)
        ktc = kt[:, tok]
        ktb = ktc.astype(BF16)
        kdb = (ktc * kdec_ref[...]).astype(BF16)
        kv = []
        for h in range(R_HEADS):
            vh = vb[tok, h * R_DV:(h + 1) * R_DV]
            sc = _dot(qms[h][tok], ktb) * dmat_ref[h]
            o_in.append(_dot(sc.astype(BF16), vh))
            kv.append(_dot(kdb[h * R_DK:(h + 1) * R_DK], vh))
        states.append(states[-1] * cdec_ref[...] + jnp.concatenate(kv, axis=0))
    for c in range(nchunk):
        tok = slice(c * chunk, (c + 1) * chunk)
        sb = states[c].astype(BF16)
        for h in range(R_HEADS):
            hs = slice(h * R_DV, (h + 1) * R_DV)
            o_h = o_in[c * R_HEADS + h] + _dot(qms[h][tok], sb) * qdec_ref[:, hs]
            o_ref[tok, hs] = _ret_finish(o_h, rg[tok, hs])
    s_scr[...] = states[-1]
    s_ref[0] = states[-1]


def _ret_prompt(z, rkt, rtab, ctab, consts, s0, batch, seq):
    chunk = math.gcd(seq, R_CHUNK)
    rows = math.gcd(seq, 4 * chunk)
    nc = seq // rows
    dmat, qdec, kdec, cdec = consts
    dk = R_HEADS * R_DK
    const = lambda shape: pl.BlockSpec(shape, lambda b, c: (0,) * len(shape))
    return pl.pallas_call(
        _ret_prompt_kernel,
        grid=(batch, nc),
        in_specs=[
            pl.BlockSpec((rows, W_RETBLK), lambda b, c: (b * nc + c, Z_RET // W_RETBLK)),
            pl.BlockSpec((dk, rows), lambda b, c: (0, b * nc + c)),
            pl.BlockSpec((3, rows, LANE), lambda b, c: (0, c, 0)),
            pl.BlockSpec((2, R_DK // 2, rows), lambda b, c: (0, 0, c)),
            const((R_HEADS, chunk, chunk)), const((chunk, R_HEADS * R_DV)),
            const((dk, chunk)), const((dk, R_DV)),
            pl.BlockSpec((1, dk, R_DV), lambda b, c: (b, 0, 0)),
        ],
        out_specs=[pl.BlockSpec((rows, R_HEADS * R_DV), lambda b, c: (b * nc + c, 0)),
                   pl.BlockSpec((1, dk, R_DV), lambda b, c: (b, 0, 0))],
        out_shape=[jax.ShapeDtypeStruct((batch * seq, R_HEADS * R_DV), BF16),
                   jax.ShapeDtypeStruct((batch, dk, R_DV), F32)],
        scratch_shapes=[pltpu.VMEM((dk, R_DV), F32)],
        compiler_params=_params(("parallel", "arbitrary")),
        name="ret_prompt",
    )(z, rkt, rtab, ctab, dmat, qdec, kdec, cdec, s0)


def _ret_sample_kernel(zr_ref, rkt_ref, rtab_ref, ctab_ref, dmat_ref, qdec_ref, kdec_ref, cdec_ref,
                       s0_ref, o_ref, s_ref, ocr_scr, *, dec_seq):
    q, kt, v, rg = _ret_prepare(zr_ref, rkt_ref, rtab_ref, ctab_ref)
    n = q.shape[0]
    nseq = n // dec_seq
    ktb = kt.astype(BF16)
    kd = kt * kdec_ref[...]
    vb = v.astype(BF16)
    qms = [jnp.where(_head_lane_mask(h, n), q, 0.0).astype(BF16) for h in range(R_HEADS)]
    tok = lax.broadcasted_iota(jnp.int32, (R_DK, n), 1) // dec_seq

    for b in range(nseq):
        r0 = b * dec_seq
        sb = s0_ref[b].astype(BF16)
        qstack = jnp.concatenate([qm[r0:r0 + dec_seq] for qm in qms], axis=0)
        res = _dot(qstack, sb)
        for h in range(R_HEADS):
            ocr_scr[r0:r0 + dec_seq, h * R_DV:(h + 1) * R_DV] = res[h * dec_seq:(h + 1) * dec_seq]

    for h in range(R_HEADS):
        vh = vb[:, h * R_DV:(h + 1) * R_DV]
        hs = slice(h * R_DV, (h + 1) * R_DV)
        sc = _dot(qms[h], ktb) * dmat_ref[h]
        o_h = _dot(sc.astype(BF16), vh) + ocr_scr[:, hs] * qdec_ref[:, hs]
        o_ref[:, hs] = _ret_finish(o_h, rg[:, hs])
        rows = slice(h * R_DK, (h + 1) * R_DK)
        kdh = kd[rows]
        for b in range(nseq):
            kb = jnp.where(tok == b, kdh, 0.0).astype(BF16)
            s_ref[b, rows, :] = s0_ref[b, rows, :] * cdec_ref[rows, :] + _dot(kb, vh)


def _ret_sample(z, rkt, rtab, ctab, consts, s0, dec_seq):
    t = z.shape[0]
    n = min(LANE, t)
    nseq = n // dec_seq
    dmat, qdec, kdec, cdec = consts
    dk = R_HEADS * R_DK
    const = lambda shape: pl.BlockSpec(shape, lambda s: (0,) * len(shape))
    return pl.pallas_call(
        functools.partial(_ret_sample_kernel, dec_seq=dec_seq),
        grid=(t // n,),
        in_specs=[
            pl.BlockSpec((n, W_RETBLK), lambda s: (s, Z_RET // W_RETBLK)),
            pl.BlockSpec((dk, n), lambda s: (0, s)),
            pl.BlockSpec((3, n, LANE), lambda s: (0, s, 0)),
            pl.BlockSpec((2, R_DK // 2, n), lambda s: (0, 0, s)),
            const((R_HEADS, n, n)), const((n, R_HEADS * R_DV)),
            const((dk, n)), const((dk, R_DV)),
            pl.BlockSpec((nseq, dk, R_DV), lambda s: (s, 0, 0)),
        ],
        out_specs=[pl.BlockSpec((n, R_HEADS * R_DV), lambda s: (s, 0)),
                   pl.BlockSpec((nseq, dk, R_DV), lambda s: (s, 0, 0))],
        out_shape=[jax.ShapeDtypeStruct((t, R_HEADS * R_DV), BF16),
                   jax.ShapeDtypeStruct((t // dec_seq, dk, R_DV), F32)],
        scratch_shapes=[pltpu.VMEM((n, R_HEADS * R_DV), F32)],
        compiler_params=_params(("parallel",)),
        name="ret_sample",
    )(z, rkt, rtab, ctab, dmat, qdec, kdec, cdec, s0)


def _merge_tail(y_conv, gate_ref, omla_ref, oret_ref, x_ref, wco_ref, wmo_ref, wro_ref, wo_ref, y_ref):
    br_a = _dot(y_conv.astype(BF16), wco_ref[...])
    br_b = _dot(omla_ref[...], wmo_ref[...])
    br_c = _dot(oret_ref[...], wro_ref[...])
    mixed = (jax.nn.sigmoid(gate_ref[:, :D_MODEL]) * br_a
             + jax.nn.sigmoid(gate_ref[:, D_MODEL:2 * D_MODEL]) * br_b
             + jax.nn.sigmoid(gate_ref[:, 2 * D_MODEL:]) * br_c)
    y_ref[...] = x_ref[...] + _dot(mixed.astype(BF16), wo_ref[...])


def _merge_prompt_kernel(zc_ref, gate_ref, omla_ref, oret_ref, x_ref, st_ref, wc_ref,
                         wco_ref, wmo_ref, wro_ref, wo_ref, y_ref, nb_ref, ext_scr):
    tm = zc_ref.shape[0]

    @pl.when(pl.program_id(1) == 0)
    def _():
        ext_scr[0:SUBLANE, :] = st_ref[0]

    cb = zc_ref[:, :CONV_DIM]
    u = zc_ref[:, CONV_DIM:2 * CONV_DIM] * zc_ref[:, 2 * CONV_DIM:]
    ext_scr[SUBLANE:SUBLANE + tm, :] = u
    conv = (ext_scr[SUBLANE - 2:SUBLANE - 2 + tm, :] * wc_ref[0:1, :]
            + ext_scr[SUBLANE - 1:SUBLANE - 1 + tm, :] * wc_ref[1:2, :]
            + u * wc_ref[2:3, :])
    tail = ext_scr[tm:tm + SUBLANE, :]
    ext_scr[0:SUBLANE, :] = tail
    nb_ref[0] = tail
    _merge_tail(cb * conv, gate_ref, omla_ref, oret_ref, x_ref, wco_ref, wmo_ref, wro_ref, wo_ref, y_ref)


def _merge_sample_kernel(zc_ref, gate_ref, omla_ref, oret_ref, x_ref, st0_ref, st1_ref, wc_ref,
                         wco_ref, wmo_ref, wro_ref, wo_ref, y_ref, u_ref, *, dec_seq):
    cb = zc_ref[:, :CONV_DIM]
    u = zc_ref[:, CONV_DIM:2 * CONV_DIM] * zc_ref[:, 2 * CONV_DIM:]
    u_ref[...] = u
    pos = lax.broadcasted_iota(jnp.int32, u.shape, 0) % dec_seq
    um1 = jnp.where(pos == 0, st1_ref[...], pltpu.roll(u, 1, 0))
    um2 = jnp.where(pos == 0, st0_ref[...], jnp.where(pos == 1, st1_ref[...], pltpu.roll(u, 2, 0)))
    conv = um2 * wc_ref[0:1, :] + um1 * wc_ref[1:2, :] + u * wc_ref[2:3, :]
    _merge_tail(cb * conv, gate_ref, omla_ref, oret_ref, x_ref, wco_ref, wmo_ref, wro_ref, wo_ref, y_ref)


def _merge_weight_specs(ngrid):
    const = lambda shape: pl.BlockSpec(shape, lambda *_: (0,) * len(shape))
    return [const((SUBLANE, CONV_DIM)), const((CONV_DIM, D_MODEL)), const((HEADS * VDIM, D_MODEL)),
            const((R_HEADS * R_DV, D_MODEL)), const((D_MODEL, D_MODEL))]


def _merge_prompt(z, omla, oret, x, st_pad, weights, batch, seq):
    tm = min(512, seq)
    nt = seq // tm
    row = lambda b, i: b * nt + i
    return pl.pallas_call(
        _merge_prompt_kernel,
        grid=(batch, nt),
        in_specs=[
            pl.BlockSpec((tm, W_CONVBLK), lambda b, i: (row(b, i), Z_CONV // W_CONVBLK)),
            pl.BlockSpec((tm, 3 * D_MODEL), lambda b, i: (row(b, i), Z_GATE // (3 * D_MODEL))),
            pl.BlockSpec((tm, HEADS * VDIM), lambda b, i: (row(b, i), 0)),
            pl.BlockSpec((tm, R_HEADS * R_DV), lambda b, i: (row(b, i), 0)),
            pl.BlockSpec((tm, D_MODEL), lambda b, i: (row(b, i), 0)),
            pl.BlockSpec((1, SUBLANE, CONV_DIM), lambda b, i: (b, 0, 0)),
        ] + _merge_weight_specs(2),
        out_specs=[pl.BlockSpec((tm, D_MODEL), lambda b, i: (row(b, i), 0)),
                   pl.BlockSpec((1, SUBLANE, CONV_DIM), lambda b, i: (b, 0, 0))],
        out_shape=[jax.ShapeDtypeStruct((batch * seq, D_MODEL), F32),
                   jax.ShapeDtypeStruct((batch, SUBLANE, CONV_DIM), F32)],
        scratch_shapes=[pltpu.VMEM((tm + SUBLANE, CONV_DIM), F32)],
        compiler_params=_params(("parallel", "arbitrary")),
        name="merge_prompt",
    )(z, z, omla, oret, x, st_pad, *weights)


def _merge_sample(z, omla, oret, x, st0, st1, weights, dec_seq):
    t = z.shape[0]
    tm = min(512, t)
    return pl.pallas_call(
        functools.partial(_merge_sample_kernel, dec_seq=dec_seq),
        grid=(t // tm,),
        in_specs=[
            pl.BlockSpec((tm, W_CONVBLK), lambda i: (i, Z_CONV // W_CONVBLK)),
            pl.BlockSpec((tm, 3 * D_MODEL), lambda i: (i, Z_GATE // (3 * D_MODEL))),
            pl.BlockSpec((tm, HEADS * VDIM), lambda i: (i, 0)),
            pl.BlockSpec((tm, R_HEADS * R_DV), lambda i: (i, 0)),
            pl.BlockSpec((tm, D_MODEL), lambda i: (i, 0)),
            pl.BlockSpec((tm, CONV_DIM), lambda i: (i, 0)),
            pl.BlockSpec((tm, CONV_DIM), lambda i: (i, 0)),
        ] + _merge_weight_specs(1),
        out_specs=[pl.BlockSpec((tm, D_MODEL), lambda i: (i, 0)),
                   pl.BlockSpec((tm, CONV_DIM), lambda i: (i, 0))],
        out_shape=[jax.ShapeDtypeStruct((t, D_MODEL), F32),
                   jax.ShapeDtypeStruct((t, CONV_DIM), F32)],
        compiler_params=_params(("parallel",)),
        name="merge_sample",
    )(z, z, omla, oret, x, st0, st1, *weights)


def _ffn_kernel(x_ref, g_ref, wu_ref, wd_ref, y_ref, h_scr, acc_scr):
    j = pl.program_id(1)

    @pl.when(j == 0)
    def _():
        x = x_ref[...]
        ms = jnp.mean(x * x, axis=-1, keepdims=True)
        h_scr[...] = (x * lax.rsqrt(ms + EPS) * g_ref[...]).astype(BF16)
        acc_scr[...] = x

    a = jnp.maximum(_dot(h_scr[...], wu_ref[...]), 0.0)
    acc_scr[...] += _dot((a * a).astype(BF16), wd_ref[...])

    @pl.when(j == pl.num_programs(1) - 1)
    def _():
        y_ref[...] = acc_scr[...]


def _ffn(x, g, wu, wd):
    t = x.shape[0]
    tm = min(1024, t)
    tf = 1024
    return pl.pallas_call(
        _ffn_kernel,
        grid=(t // tm, D_FF // tf),
        in_specs=[
            pl.BlockSpec((tm, D_MODEL), lambda i, j: (i, 0)),
            pl.BlockSpec((1, D_MODEL), lambda i, j: (0, 0)),
            pl.BlockSpec((D_MODEL, tf), lambda i, j: (0, j)),
            pl.BlockSpec((tf, D_MODEL), lambda i, j: (j, 0)),
        ],
        out_specs=pl.BlockSpec((tm, D_MODEL), lambda i, j: (i, 0)),
        out_shape=jax.ShapeDtypeStruct((t, D_MODEL), F32),
        scratch_shapes=[pltpu.VMEM((tm, D_MODEL), BF16), pltpu.VMEM((tm, D_MODEL), F32)],
        compiler_params=_params(("parallel", "arbitrary")),
        name="ffn",
    )(x, g, wu, wd)


def _rope_tables(pos):
    posf = pos.astype(F32)[:, None]
    n = pos.shape[0]

    def cs(half):
        inv = THETA ** (-jnp.arange(half, dtype=F32) / half)
        ang = posf * inv[None, :]
        return jnp.cos(ang), jnp.sin(ang)

    c16, s16 = cs(ROPE // 2)
    one = lambda w: jnp.ones((n, w), F32)
    zero = lambda w: jnp.zeros((n, w), F32)
    qtab = jnp.stack([
        jnp.concatenate([one(NOPE), c16, c16, one(HT - QK)], axis=1),
        jnp.concatenate([zero(NOPE), -s16, zero(HT - NOPE - ROPE // 2)], axis=1),
        jnp.concatenate([zero(NOPE + ROPE // 2), s16, zero(HT - QK)], axis=1)])
    ktab = jnp.stack([
        jnp.concatenate([c16, c16, one(LANE - ROPE)], axis=1),
        jnp.concatenate([-s16, zero(LANE - ROPE // 2)], axis=1),
        jnp.concatenate([zero(ROPE // 2), s16, zero(LANE - ROPE)], axis=1)])
    c32, s32 = cs(R_DK // 2)
    z32 = zero(R_DK // 2)
    rtab = jnp.stack([
        jnp.concatenate([c32] * (LANE // (R_DK // 2)), axis=1),
        jnp.concatenate([-s32, z32] * (LANE // R_DK), axis=1),
        jnp.concatenate([z32, s32] * (LANE // R_DK), axis=1)])
    ctab = jnp.stack([c32.T, s32.T])
    return qtab, ktab, rtab, ctab


def _ret_consts(n, chunk):
    log_g = jnp.log1p(-(2.0 ** (-5.0 - jnp.arange(R_HEADS, dtype=F32))))
    idx = jnp.arange(n)
    loc = (idx % chunk).astype(F32)
    same = (idx[:, None] // chunk) == (idx[None, :] // chunk)
    diff = loc[:, None] - loc[None, :]
    dmat = jnp.where(same & (diff >= 0), jnp.exp(log_g[:, None, None] * jnp.maximum(diff, 0.0)), 0.0)
    qd = jnp.exp(log_g[None, :] * (loc[:, None] + 1.0))
    qdec = jnp.repeat(qd, R_DV, axis=1)
    kd = jnp.exp(log_g[:, None] * (chunk - 1.0 - loc[None, :]))
    kdec = jnp.repeat(kd, R_DK, axis=0)
    cdec = jnp.repeat(jnp.exp(log_g * chunk)[:, None], R_DK, axis=0) * jnp.ones((1, R_DV), F32)
    return dmat.astype(F32), qdec, kdec, cdec


def _layer_weights(l, g_mix, w_in, w_conv, w_conv_out, g_q_lat, w_uq, g_kv_lat, w_ukv, g_qn, g_kn,
                   w_mla_out, w_ret_out, w_o, g_ffn, w_up, w_down):
    wi = w_in[l]
    o_cq, o_rq, o_rk, o_gl, o_end = 1536, 2208, 2464, 3744, 6816
    w_perm = jnp.concatenate(
        [wi[:, :o_cq], wi[:, o_rq:o_gl], wi[:, o_gl:o_end], wi[:, o_cq:o_rq],
         jnp.zeros((D_MODEL, LANE - ROPE), F32)], axis=1).astype(BF16)
    w_rk_t = wi[:, o_rk:o_rk + R_HEADS * R_DK].T.astype(BF16)
    wuq = jnp.pad(w_uq[l].reshape(Q_LORA, HEADS, QK), ((0, 0), (0, 0), (0, HT - QK)))
    wuq = wuq.reshape(Q_LORA, HEADS * HT).astype(BF16)
    wkv = w_ukv[l].reshape(KV_LORA, HEADS, NOPE + VDIM)
    w_uk, w_uv = wkv[..., :NOPE], wkv[..., NOPE:]
    wk_pad = jnp.pad(w_uk, ((0, 0), (0, 0), (0, HT - NOPE))).reshape(KV_LORA, HEADS * HT).astype(BF16)
    wv = w_uv.reshape(KV_LORA, HEADS * VDIM).astype(BF16)
    wv_pad = jnp.pad(w_uv, ((0, 0), (0, 0), (0, HT - VDIM))).reshape(KV_LORA, HEADS * HT).T.astype(BF16)
    wukt = w_uk.reshape(KV_LORA, HEADS * NOPE).T.astype(BF16)
    j = np.arange(ROPE)
    pm = np.zeros((LANE, HEADS * HT), np.float32)
    for h in range(HEADS):
        pm[j, h * HT + NOPE + j] = 1.0
    shift = np.zeros((HT, LANE), np.float32)
    shift[NOPE + j, j] = 1.0
    wabs = jnp.pad(jnp.transpose(w_uk, (1, 2, 0)), ((0, 0), (0, HT - NOPE), (0, 0)))
    wcomb = jnp.concatenate([wabs, jnp.broadcast_to(jnp.asarray(shift), (HEADS, HT, LANE))],
                            axis=2).astype(BF16)
    pad_gain = lambda g: jnp.pad(g, (0, HT - QK))[None, :]
    return dict(
        g_mix=g_mix[l][None, :], w_perm=w_perm, w_rk_t=w_rk_t,
        g_q_lat=g_q_lat[l][None, :], wuq=wuq, g_kv_lat=g_kv_lat[l][None, :],
        gqn=pad_gain(g_qn[l]) * (QK ** -0.5 * math.log2(math.e)), gkn=pad_gain(g_kn[l]),
        wk_pad=wk_pad, pmat=jnp.asarray(pm, BF16), wv=wv, wv_pad=wv_pad, wukt=wukt, wcomb=wcomb,
        merge=(jnp.pad(w_conv[l], ((0, SUBLANE - CONV_W), (0, 0))), w_conv_out[l].astype(BF16),
               w_mla_out[l].astype(BF16), w_ret_out[l].astype(BF16), w_o[l].astype(BF16)),
        g_ffn=g_ffn[l][None, :], w_up=w_up[l].astype(BF16), w_down=w_down[l].astype(BF16))


def kernel(x_prompt, x_sample, cache_ckv, cache_kpe, state_conv, state_ret, page_table, g_mix, w_in,
           w_conv, w_conv_out, g_q_lat, w_uq, g_kv_lat, w_ukv, g_qn, g_kn, w_mla_out, w_ret_out, w_o,
           g_ffn, w_up, w_down):
    batch, seq, _ = x_prompt.shape
    nb, dec_seq, _ = x_sample.shape
    depth = w_in.shape[0]
    past = page_table.shape[1] * PAGE
    assert dec_seq == SUBLANE
    tp, ts = batch * seq, nb * dec_seq
    dk = R_HEADS * R_DK

    tabs_p = _rope_tables(jnp.arange(seq))
    tabs_s = _rope_tables(jnp.tile(past + jnp.arange(dec_seq), nb))
    chunk_p = math.gcd(seq, R_CHUNK)
    consts_p = _ret_consts(chunk_p, chunk_p)
    consts_s = _ret_consts(min(LANE, ts), dec_seq)
    tm_p, tm_s = min(512, seq), min(512, ts)

    cache_kpe_t = jnp.swapaxes(cache_kpe, 2, 3)
    yp = x_prompt.reshape(tp, D_MODEL)
    ys = x_sample.reshape(ts, D_MODEL)
    conv0 = jnp.zeros((batch, SUBLANE, CONV_DIM), F32)
    ret0 = jnp.zeros((batch, dk, R_DV), F32)
    outs = {k: [] for k in ("ckv_p", "kpe_p", "conv_p", "ret_p", "ckv_s", "kpe_s", "conv_s", "ret_s")}

    for l in range(depth):
        w = _layer_weights(l, g_mix, w_in, w_conv, w_conv_out, g_q_lat, w_uq, g_kv_lat, w_ukv, g_qn,
                           g_kn, w_mla_out, w_ret_out, w_o, g_ffn, w_up, w_down)

        qtab, ktab, rtab, ctab = tabs_p
        z, rkt = _inproj(yp, w["g_mix"], w["w_perm"], w["w_rk_t"])
        q, ckvn, kper, k, v = _mlaprep(z, qtab, ktab, tm_p, w["g_q_lat"], w["wuq"], w["g_kv_lat"],
                                       w["gqn"], (w["wk_pad"], w["pmat"], w["wv_pad"], w["gkn"]))
        omla = _flash(q, k, v, batch, seq)
        oret, snew = _ret_prompt(z, rkt, rtab, ctab, consts_p, ret0, batch, seq)
        ymid, nbuf = _merge_prompt(z, omla, oret, yp, conv0, w["merge"], batch, seq)
        yp = _ffn(ymid, w["g_ffn"], w["w_up"], w["w_down"])
        outs["ckv_p"].append(ckvn.reshape(batch, seq, KV_LORA))
        outs["kpe_p"].append(kper.reshape(batch, seq, ROPE))
        outs["conv_p"].append(nbuf[:, SUBLANE - (CONV_W - 1):, :])
        outs["ret_p"].append(snew.reshape(batch, R_HEADS, R_DK, R_DV))

        qtab, ktab, rtab, ctab = tabs_s
        z, rkt = _inproj(ys, w["g_mix"], w["w_perm"], w["w_rk_t"])
        q, ckvn, kper = _mlaprep(z, qtab, ktab, tm_s, w["g_q_lat"], w["wuq"], w["g_kv_lat"],
                                 w["gqn"], None)
        qa, qp = _qabs(q, w["gkn"], w["wcomb"], nb, dec_seq)
        cnew = jnp.pad(ckvn.reshape(nb, dec_seq, KV_LORA), ((0, 0), (0, PAGE - dec_seq), (0, 0)))
        pnew_t = jnp.pad(jnp.swapaxes(kper.reshape(nb, dec_seq, ROPE), 1, 2),
                         ((0, 0), (0, 0), (0, PAGE - dec_seq)))
        omla = _decode(page_table, l, cache_ckv, cache_kpe_t,
                       qa.reshape(nb, HEADS * dec_seq, KV_LORA), qp.reshape(nb, HEADS * dec_seq, LANE),
                       w["wukt"], w["wv"], cnew, pnew_t).reshape(ts, HEADS * VDIM)
        oret, snew = _ret_sample(z, rkt, rtab, ctab, consts_s, state_ret[l].reshape(nb, dk, R_DV), dec_seq)
        st = state_conv[l]
        st0 = jnp.repeat(st[:, 0, :], dec_seq, axis=0)
        st1 = jnp.repeat(st[:, 1, :], dec_seq, axis=0)
        ymid, u = _merge_sample(z, omla, oret, ys, st0, st1, w["merge"], dec_seq)
        ys = _ffn(ymid, w["g_ffn"], w["w_up"], w["w_down"])
        outs["ckv_s"].append(ckvn.reshape(nb, dec_seq, KV_LORA))
        outs["kpe_s"].append(kper.reshape(nb, dec_seq, ROPE))
        outs["conv_s"].append(u.reshape(nb, dec_seq, CONV_DIM)[:, dec_seq - (CONV_W - 1):, :])
        outs["ret_s"].append(snew.reshape(nb, R_HEADS, R_DK, R_DV))

    return (yp.reshape(batch, seq, D_MODEL), ys.reshape(nb, dec_seq, D_MODEL),
            jnp.stack(outs["ckv_p"]), jnp.stack(outs["kpe_p"]), jnp.stack(outs["conv_p"]),
            jnp.stack(outs["ret_p"]),
            jnp.stack(outs["ckv_s"]), jnp.stack(outs["kpe_s"]), jnp.stack(outs["conv_s"]),
            jnp.stack(outs["ret_s"]))
```

```python
import functools
import math

import jax
import jax.numpy as jnp
import numpy as np
from jax import lax
from jax.experimental import pallas as pl
from jax.experimental.pallas import tpu as pltpu

F32 = jnp.float32
BF16 = jnp.bfloat16

D_MODEL = 1024
PAGE = 128
CONV_DIM = 512
CONV_W = 3
HEADS = 8
NOPE = 64
ROPE = 32
QK = NOPE + ROPE
VDIM = 64
Q_LORA = 384
KV_LORA = 256
R_HEADS = 4
R_DK = 64
R_DV = 128
R_CHUNK = 128
D_FF = 4096
THETA = 10000.0
EPS = 1e-6
NEG = -1e30

LANE = 128
SUBLANE = 8
HT = LANE

Z_CONV = 0
Z_RET = 1536
Z_GATE = 3072
Z_MLA = 6144
Z_W = 6912
W_CONVBLK = 3 * CONV_DIM
W_RETBLK = 2 * R_HEADS * R_DK + 2 * R_HEADS * R_DV
W_MLABLK = Q_LORA + KV_LORA + LANE

VMEM_LIMIT = 56 * 1024 * 1024


def _dot(a, b):
    return jnp.dot(a, b, preferred_element_type=F32)


def _dot_nt(a, b):
    return lax.dot_general(a, b, (((1,), (1,)), ((), ())), preferred_element_type=F32)


def _params(sem):
    return pltpu.CompilerParams(dimension_semantics=sem, vmem_limit_bytes=VMEM_LIMIT)


def _rope_rows(x, tab_ref, half):
    return (x * tab_ref[0] + pltpu.roll(x, LANE - half, 1) * tab_ref[1]
            + pltpu.roll(x, half, 1) * tab_ref[2])


def _inproj_kernel(x_ref, g_ref, w_ref, wrk_ref, z_ref, rkt_ref, h_scr):
    @pl.when(pl.program_id(1) == 0)
    def _():
        x = x_ref[...]
        ms = jnp.mean(x * x, axis=-1, keepdims=True)
        hb = (x * lax.rsqrt(ms + EPS) * g_ref[...]).astype(BF16)
        h_scr[...] = hb
        rkt_ref[...] = _dot_nt(wrk_ref[...], hb)

    z_ref[...] = _dot(h_scr[...], w_ref[...])


def _inproj(x, g, w_perm, w_rk_t):
    t = x.shape[0]
    tm = min(1024, t)
    tn = 2304
    return pl.pallas_call(
        _inproj_kernel,
        grid=(t // tm, Z_W // tn),
        in_specs=[
            pl.BlockSpec((tm, D_MODEL), lambda i, j: (i, 0)),
            pl.BlockSpec((1, D_MODEL), lambda i, j: (0, 0)),
            pl.BlockSpec((D_MODEL, tn), lambda i, j: (0, j)),
            pl.BlockSpec((R_HEADS * R_DK, D_MODEL), lambda i, j: (0, 0)),
        ],
        out_specs=[
            pl.BlockSpec((tm, tn), lambda i, j: (i, j)),
            pl.BlockSpec((R_HEADS * R_DK, tm), lambda i, j: (0, i)),
        ],
        out_shape=[jax.ShapeDtypeStruct((t, Z_W), F32),
                   jax.ShapeDtypeStruct((R_HEADS * R_DK, t), F32)],
        scratch_shapes=[pltpu.VMEM((tm, D_MODEL), BF16)],
        compiler_params=_params(("parallel", "arbitrary")),
        name="inproj",
    )(x, g, w_perm, w_rk_t)


def _mlaprep_kernel(z_ref, qtab_ref, ktab_ref, gq_ref, wuq_ref, gkv_ref, gqn_ref,
                    *rest, expand):
    if expand:
        wk_ref, pmat_ref, wv_ref, gkn_ref, q_ref, ckv_ref, kpe_ref, k_ref, v_ref = rest
    else:
        q_ref, ckv_ref, kpe_ref = rest
    cq = z_ref[:, :Q_LORA]
    ckv = z_ref[:, Q_LORA:Q_LORA + KV_LORA]
    kpe = z_ref[:, Q_LORA + KV_LORA:]

    cqn = cq * lax.rsqrt(jnp.mean(cq * cq, axis=-1, keepdims=True) + EPS) * gq_ref[...]
    qf = _dot(cqn.astype(BF16), wuq_ref[...])
    for h in range(HEADS):
        xr = _rope_rows(qf[:, h * HT:(h + 1) * HT], qtab_ref, ROPE // 2)
        ms = jnp.sum(xr * xr, axis=-1, keepdims=True) * (1.0 / QK)
        q_ref[:, h * HT:(h + 1) * HT] = (xr * lax.rsqrt(ms + EPS) * gqn_ref[...]).astype(BF16)

    ckvn = ckv * lax.rsqrt(jnp.mean(ckv * ckv, axis=-1, keepdims=True) + EPS) * gkv_ref[...]
    ckv_ref[...] = ckvn
    kper = _rope_rows(kpe, ktab_ref, ROPE // 2)
    kpe_ref[...] = kper[:, :ROPE]

    if expand:
        cb = ckvn.astype(BF16)
        p_hi = kper.astype(BF16)
        p_lo = (kper - p_hi.astype(F32)).astype(BF16)
        kf = _dot(cb, wk_ref[...]) + _dot(p_hi, pmat_ref[...]) + _dot(p_lo, pmat_ref[...])
        for h in range(HEADS):
            xk = kf[:, h * HT:(h + 1) * HT]
            ms = jnp.sum(xk * xk, axis=-1, keepdims=True) * (1.0 / QK)
            k_ref[:, h * HT:(h + 1) * HT] = (xk * lax.rsqrt(ms + EPS) * gkn_ref[...]).astype(BF16)
        vt = _dot_nt(wv_ref[...], cb)
        row = lax.broadcasted_iota(jnp.int32, vt.shape, 0) % HT
        v_ref[...] = jnp.where(row == VDIM, 1.0, vt).astype(BF16)


def _mlaprep(z, qtab, ktab, tm, gq, wuq, gkv, gqn, expand_args):
    t = z.shape[0]
    n_seq_tiles = qtab.shape[1] // tm
    expand = expand_args is not None
    const = lambda shape: pl.BlockSpec(shape, lambda i: (0,) * len(shape))
    tab = pl.BlockSpec((3, tm, LANE), lambda i: (0, i % n_seq_tiles, 0))
    in_specs = [
        pl.BlockSpec((tm, W_MLABLK), lambda i: (i, Z_MLA // W_MLABLK)),
        tab, tab,
        const((1, Q_LORA)), const((Q_LORA, HEADS * HT)), const((1, KV_LORA)), const((1, HT)),
    ]
    args = [z, qtab, ktab, gq, wuq, gkv, gqn]
    out_specs = [pl.BlockSpec((tm, HEADS * HT), lambda i: (i, 0)),
                 pl.BlockSpec((tm, KV_LORA), lambda i: (i, 0)),
                 pl.BlockSpec((tm, ROPE), lambda i: (i, 0))]
    out_shape = [jax.ShapeDtypeStruct((t, HEADS * HT), BF16),
                 jax.ShapeDtypeStruct((t, KV_LORA), F32),
                 jax.ShapeDtypeStruct((t, ROPE), F32)]
    if expand:
        wk, pmat, wv, gkn = expand_args
        in_specs += [const((KV_LORA, HEADS * HT)), const((LANE, HEADS * HT)),
                     const((HEADS * HT, KV_LORA)), const((1, HT))]
        args += [wk, pmat, wv, gkn]
        out_specs += [pl.BlockSpec((tm, HEADS * HT), lambda i: (i, 0)),
                      pl.BlockSpec((HEADS * HT, tm), lambda i: (0, i))]
        out_shape += [jax.ShapeDtypeStruct((t, HEADS * HT), BF16),
                      jax.ShapeDtypeStruct((HEADS * HT, t), BF16)]
    return pl.pallas_call(
        functools.partial(_mlaprep_kernel, expand=expand),
        grid=(t // tm,),
        in_specs=in_specs, out_specs=out_specs, out_shape=out_shape,
        compiler_params=_params(("parallel",)),
        name="mlaprep_expand" if expand else "mlaprep",
    )(*args)


def _flash_kernel(q_ref, k_ref, vt_ref, o_ref, sa_scr, sb_scr, m_scr, acc_scr, *, tq, tk):
    qi = pl.program_id(2)
    m_scr[...] = jnp.full(m_scr.shape, NEG, F32)
    acc_scr[...] = jnp.zeros(acc_scr.shape, F32)

    def scores(ki, s_scr):
        start = pl.multiple_of(ki * tk, tk)
        for h in range(2):
            hs = slice(h * HT, (h + 1) * HT)
            s_scr[h] = _dot_nt(k_ref[pl.ds(start, tk), hs], q_ref[:, hs])

    def attend(ki, s_scr, diag):
        start = pl.multiple_of(ki * tk, tk)
        for h in range(2):
            s = s_scr[h]
            if diag is not None:
                key = lax.broadcasted_iota(jnp.int32, (tk, tq), 0) + diag * tk
                qry = lax.broadcasted_iota(jnp.int32, (tk, tq), 1)
                s = jnp.where(key <= qry, s, NEG)
            m = m_scr[h]
            m_new = jnp.maximum(m, jnp.max(s, axis=0, keepdims=True))
            p = jnp.exp(s - m_new).astype(BF16)
            vt = vt_ref[h * HT:(h + 1) * HT, pl.ds(start, tk)]
            acc_scr[h] = jnp.exp(m - m_new) * acc_scr[h] + _dot(vt, p)
            m_scr[h] = m_new

    def body(j, carry):
        scores(2 * j + 1, sb_scr)
        attend(2 * j, sa_scr, None)
        scores(2 * j + 2, sa_scr)
        attend(2 * j + 1, sb_scr, None)
        return carry

    scores(0, sa_scr)
    lax.fori_loop(0, qi, body, 0)
    scores(2 * qi + 1, sb_scr)
    attend(2 * qi, sa_scr, 0)
    attend(2 * qi + 1, sb_scr, 1)
    outs = []
    for h in range(2):
        a = acc_scr[h]
        outs.append((a / a[VDIM:VDIM + 1, :]).T)
    lane = lax.broadcasted_iota(jnp.int32, (tq, LANE), 1)
    o_ref[...] = jnp.where(lane < VDIM, outs[0], pltpu.roll(outs[1], VDIM, 1)).astype(BF16)


def _flash(q, k, vt, batch, seq):
    tq = min(512, seq)
    tk = tq // 2
    nq = seq // tq
    return pl.pallas_call(
        functools.partial(_flash_kernel, tq=tq, tk=tk),
        grid=(batch, HEADS // 2, nq),
        in_specs=[
            pl.BlockSpec((tq, 2 * HT), lambda b, hp, i: (b * nq + i, hp)),
            pl.BlockSpec((seq, 2 * HT), lambda b, hp, i: (b, hp)),
            pl.BlockSpec((2 * HT, seq), lambda b, hp, i: (hp, b)),
        ],
        out_specs=pl.BlockSpec((tq, 2 * VDIM), lambda b, hp, i: (b * nq + i, hp)),
        out_shape=jax.ShapeDtypeStruct((batch * seq, HEADS * VDIM), BF16),
        scratch_shapes=[pltpu.VMEM((2, tk, tq), F32), pltpu.VMEM((2, tk, tq), F32),
                        pltpu.VMEM((2, 1, tq), F32), pltpu.VMEM((2, HT, tq), F32)],
        compiler_params=_params(("parallel", "parallel", "arbitrary")),
        name="flash",
    )(q, k, vt)


def _qabs_kernel(q_ref, gkn_ref, wcomb_ref, qa_ref, qp_ref):
    nb = qa_ref.shape[0]
    for h in range(HEADS):
        qg = (q_ref[:, h * HT:(h + 1) * HT].astype(F32) * gkn_ref[...]).astype(BF16)
        r = _dot(qg, wcomb_ref[h])
        qa_ref[:, h] = r[:, :KV_LORA].reshape(nb, SUBLANE, KV_LORA)
        qp_ref[:, h] = r[:, KV_LORA:].reshape(nb, SUBLANE, LANE)


def _qabs(q, gkn, wcomb, nb, dec_seq):
    t = q.shape[0]
    return pl.pallas_call(
        _qabs_kernel,
        grid=(1,),
        in_specs=[pl.BlockSpec((t, HEADS * HT), lambda i: (0, 0)),
                  pl.BlockSpec((1, HT), lambda i: (0, 0)),
                  pl.BlockSpec((HEADS, HT, KV_LORA + LANE), lambda i: (0, 0, 0))],
        out_specs=[pl.BlockSpec((nb, HEADS, dec_seq, KV_LORA), lambda i: (0, 0, 0, 0)),
                   pl.BlockSpec((nb, HEADS, dec_seq, LANE), lambda i: (0, 0, 0, 0))],
        out_shape=[jax.ShapeDtypeStruct((nb, HEADS, dec_seq, KV_LORA), F32),
                   jax.ShapeDtypeStruct((nb, HEADS, dec_seq, LANE), F32)],
        compiler_params=_params(("arbitrary",)),
        name="qabs",
    )(q, gkn, wcomb)


DEC_ROWS = HEADS * SUBLANE
DEC_KROWS = 3 * ROPE + ROPE
DEC_PROWS = DEC_ROWS + 16


def _decode_kernel(pt_ref, qa_ref, qp_ref, wukt_ref, wv_ref, cnew_ref, pnew_ref, ckv_hbm, kpe_hbm,
                   o_ref, cbuf, pbuf, sem, cbf, kt, wall, lpe, s_scr, m_scr, l_scr, acc_scr,
                   *, npg, sub, grp, layer):
    b = pl.program_id(0)
    g = pl.program_id(1)
    ng = pl.num_programs(1)
    step = b * ng + g
    slot = step % 2
    nk = HEADS * NOPE

    def page_copies(bb, gg, sl):
        copies = []
        for i in range(npg):
            pid = 0 if bb is None else pt_ref[bb, gg * npg + i]
            copies.append(pltpu.make_async_copy(ckv_hbm.at[layer, pid], cbuf.at[sl, i], sem.at[0, sl]))
            copies.append(pltpu.make_async_copy(kpe_hbm.at[layer, pid], pbuf.at[sl, i], sem.at[1, sl]))
        return copies

    @pl.when(step == 0)
    def _():
        for cp in page_copies(b, g, slot):
            cp.start()

    wrap = g + 1 == ng
    b_next = jnp.where(wrap, jnp.where(b + 1 == pl.num_programs(0), 0, b + 1), b)
    prefetch = page_copies(b_next, jnp.where(wrap, 0, g + 1), 1 - slot)
    for cp in prefetch:
        cp.start()

    for cp in page_copies(None, None, slot):
        cp.wait()
    c_pages = [cbuf.at[slot, i] for i in range(npg)]
    p_pages = [pbuf.at[slot, i] for i in range(npg)]

    @pl.when(g == 0)
    def _():
        m_scr[...] = jnp.full(m_scr.shape, NEG, F32)
        l_scr[...] = jnp.zeros(l_scr.shape, F32)
        acc_scr[...] = jnp.zeros(acc_scr.shape, F32)
        kt[3 * ROPE:, :] = jnp.zeros((DEC_KROWS - 3 * ROPE, kt.shape[1]), BF16)
        wall[:nk, :] = wukt_ref[...]
        wall[nk:, :] = qa_ref[0].astype(BF16)
        lane = lax.broadcasted_iota(jnp.int32, (DEC_PROWS - DEC_ROWS, DEC_KROWS), 1)
        lpe[:DEC_ROWS, :] = qp_ref[0].astype(BF16)
        lpe[DEC_ROWS:, :] = jnp.where((lane >= ROPE) & (lane < 3 * ROPE), 1.0, 0.0).astype(BF16)

    def stage(i, c, kp_t):
        cols = slice(i * PAGE, (i + 1) * PAGE)
        cbf[cols, :] = c.astype(BF16)
        kt[:ROPE, cols] = kp_t.astype(BF16)
        kp2 = kp_t * kp_t
        hi = kp2.astype(BF16)
        kt[ROPE:2 * ROPE, cols] = hi
        kt[2 * ROPE:3 * ROPE, cols] = (kp2 - hi.astype(F32)).astype(BF16)

    def scores(start, width):
        ct = cbf[start:start + width, :]
        big = _dot_nt(wall[...], ct)
        ext = _dot(lpe[...], kt[:, start:start + width])
        kpsq = ext[DEC_ROWS:DEC_ROWS + 1]
        for h in range(HEADS):
            blk = big[h * NOPE:(h + 1) * NOPE]
            nsq = jnp.sum(blk * blk, axis=0, keepdims=True)
            rs = lax.rsqrt((nsq + kpsq) * (1.0 / QK) + EPS)
            hr = slice(h * SUBLANE, (h + 1) * SUBLANE)
            s_scr[hr, start:start + width] = (big[nk + h * SUBLANE:nk + (h + 1) * SUBLANE] + ext[hr]) * rs

    def update(width, mask):
        if mask is not None:
            s_scr[:, :width] = jnp.where(mask, s_scr[:, :width], NEG)
        m = m_scr[...]
        m_new = jnp.maximum(m, jnp.max(s_scr[:, :width], axis=-1, keepdims=True))
        alpha = jnp.exp(m - m_new)
        acc = alpha * acc_scr[...]
        lsum = alpha * l_scr[...]
        chunk = min(grp, width)
        for c in range(width // chunk):
            cs = slice(c * chunk, (c + 1) * chunk)
            p = jnp.exp(s_scr[:, cs] - m_new)
            lsum = lsum + jnp.sum(p, axis=-1, keepdims=True)
            acc = acc + _dot(p.astype(BF16), cbf[cs, :])
        l_scr[...] = lsum
        acc_scr[...] = acc
        m_scr[...] = m_new

    pages_per_sub = sub // PAGE
    for j in range(npg // pages_per_sub):
        for i in range(j * pages_per_sub, (j + 1) * pages_per_sub):
            stage(i, c_pages[i][...], p_pages[i][...])
        scores(j * sub, sub)
    update(npg * PAGE, None)

    @pl.when(g == ng - 1)
    def _():
        stage(0, cnew_ref[0], pnew_ref[0])
        scores(0, PAGE)
        r = lax.broadcasted_iota(jnp.int32, (DEC_ROWS, PAGE), 0)
        t = lax.broadcasted_iota(jnp.int32, (DEC_ROWS, PAGE), 1)
        update(PAGE, t <= (r % SUBLANE))
        olat = (acc_scr[...] / l_scr[...]).astype(BF16)
        full = _dot(olat, wv_ref[...])
        colh = lax.broadcasted_iota(jnp.int32, (SUBLANE, HEADS * VDIM), 1) // VDIM
        out = jnp.zeros((SUBLANE, HEADS * VDIM), F32)
        for h in range(HEADS):
            out = out + jnp.where(colh == h, full[h * SUBLANE:(h + 1) * SUBLANE], 0.0)
        o_ref[0] = out.astype(BF16)

    @pl.when(step == pl.num_programs(0) * ng - 1)
    def _():
        for cp in prefetch:
            cp.wait()


def _decode(page_table, layer, cache_ckv, cache_kpe_t, qa, qp, wukt, wv, cnew, pnew_t):
    nb, n_pages = page_table.shape
    npg = min(32, n_pages)
    sub = min(256, npg * PAGE)
    ngrp = n_pages // npg

    in_specs = [
        pl.BlockSpec((1, DEC_ROWS, KV_LORA), lambda b, g, pt: (b, 0, 0)),
        pl.BlockSpec((1, DEC_ROWS, DEC_KROWS), lambda b, g, pt: (b, 0, 0)),
        pl.BlockSpec((HEADS * NOPE, KV_LORA), lambda b, g, pt: (0, 0)),
        pl.BlockSpec((KV_LORA, HEADS * VDIM), lambda b, g, pt: (0, 0)),
        pl.BlockSpec((1, PAGE, KV_LORA), lambda b, g, pt: (b, 0, 0)),
        pl.BlockSpec((1, ROPE, PAGE), lambda b, g, pt: (b, 0, 0)),
        pl.BlockSpec(memory_space=pl.ANY),
        pl.BlockSpec(memory_space=pl.ANY),
    ]
    grid_spec = pltpu.PrefetchScalarGridSpec(
        num_scalar_prefetch=1,
        grid=(nb, ngrp),
        in_specs=in_specs,
        out_specs=pl.BlockSpec((1, SUBLANE, HEADS * VDIM), lambda b, g, pt: (b, 0, 0)),
        scratch_shapes=[
            pltpu.VMEM((2, npg, PAGE, KV_LORA), F32),
            pltpu.VMEM((2, npg, ROPE, PAGE), F32),
            pltpu.SemaphoreType.DMA((2, 2)),
            pltpu.VMEM((npg * PAGE, KV_LORA), BF16),
            pltpu.VMEM((DEC_KROWS, npg * PAGE), BF16),
            pltpu.VMEM((HEADS * NOPE + DEC_ROWS, KV_LORA), BF16),
            pltpu.VMEM((DEC_PROWS, DEC_KROWS), BF16),
            pltpu.VMEM((DEC_ROWS, npg * PAGE), F32),
            pltpu.VMEM((DEC_ROWS, 1), F32),
            pltpu.VMEM((DEC_ROWS, 1), F32),
            pltpu.VMEM((DEC_ROWS, KV_LORA), F32),
        ],
    )
    return pl.pallas_call(
        functools.partial(_decode_kernel, npg=npg, sub=sub, grp=min(512, npg * PAGE), layer=layer),
        grid_spec=grid_spec,
        out_shape=jax.ShapeDtypeStruct((nb, SUBLANE, HEADS * VDIM), BF16),
        compiler_params=_params(("arbitrary", "arbitrary")),
        name="decode",
    )(page_table, qa, qp, wukt, wv, cnew, pnew_t, cache_ckv, cache_kpe_t)


def _ret_prepare(zr_ref, rkt_ref, rtab_ref, ctab_ref):
    dk = R_HEADS * R_DK
    q = jnp.concatenate(
        [_rope_rows(zr_ref[:, i * LANE:(i + 1) * LANE], rtab_ref, R_DK // 2) for i in range(dk // LANE)],
        axis=1)
    cos_t = ctab_ref[0]
    sin_t = ctab_ref[1]
    half = R_DK // 2
    kparts = []
    for h in range(R_HEADS):
        a = rkt_ref[h * R_DK:h * R_DK + half, :]
        b = rkt_ref[h * R_DK + half:(h + 1) * R_DK, :]
        kparts += [a * cos_t - b * sin_t, b * cos_t + a * sin_t]
    kt = jnp.concatenate(kparts, axis=0) * (R_DK ** -0.5)
    v = zr_ref[:, 2 * dk:2 * dk + R_HEADS * R_DV]
    rg = zr_ref[:, 2 * dk + R_HEADS * R_DV:]
    return q, kt, v, rg


def _ret_finish(o_h, rg_h):
    ms = jnp.mean(o_h * o_h, axis=-1, keepdims=True)
    return (jax.nn.silu(rg_h) * (o_h * lax.rsqrt(ms + EPS))).astype(BF16)


def _head_lane_mask(h, n):
    lane = lax.broadcasted_iota(jnp.int32, (n, R_HEADS * R_DK), 1)
    return (lane // R_DK) == h


def _ret_prompt_kernel(zr_ref, rkt_ref, rtab_ref, ctab_ref, dmat_ref, qdec_ref, kdec_ref, cdec_ref,
                       s0_ref, o_ref, s_ref, s_scr):
    c = pl.program_id(1)

    @pl.when(c == 0)
    def _():
        s_scr[...] = s0_ref[0]

    q, kt, v, rg = _ret_prepare(zr_ref, rkt_ref, rtab_ref, ctab_ref)
    chunk = dmat_ref.shape[1]
    nchunk = q.shape[0] // chunk
    vb = v.astype(BF16)
    qms = [jnp.where(_head_lane_mask(h, q.shape[0]), q, 0.0).astype(BF16) for h in range(R_HEADS)]
    states = [s_scr[...]]
    o_in = []
    for c in range(nchunk):
        tok = slice(c * chunk, (c + 1) * chunk)
        ktc = kt[:, tok]
        ktb = ktc.astype(BF16)
        kdb = (ktc * kdec_ref[...]).astype(BF16)
        kv = []
        for h in range(R_HEADS):
            vh = vb[tok, h * R_DV:(h + 1) * R_DV]
            sc = _dot(qms[h][tok], ktb) * dmat_ref[h]
            o_in.append(_dot(sc.astype(BF16), vh))
            kv.append(_dot(kdb[h * R_DK:(h + 1) * R_DK], vh))
        states.append(states[-1] * cdec_ref[...] + jnp.concatenate(kv, axis=0))
    for c in range(nchunk):
        tok = slice(c * chunk, (c + 1) * chunk)
        sb = states[c].astype(BF16)
        for h in range(R_HEADS):
            hs = slice(h * R_DV, (h + 1) * R_DV)
            o_h = o_in[c * R_HEADS + h] + _dot(qms[h][tok], sb) * qdec_ref[:, hs]
            o_ref[tok, hs] = _ret_finish(o_h, rg[tok, hs])
    s_scr[...] = states[-1]
    s_ref[0] = states[-1]


def _ret_prompt(z, rkt, rtab, ctab, consts, s0, batch, seq):
    chunk = math.gcd(seq, R_CHUNK)
    rows = math.gcd(seq, 4 * chunk)
    nc = seq // rows
    dmat, qdec, kdec, cdec = consts
    dk = R_HEADS * R_DK
    const = lambda shape: pl.BlockSpec(shape, lambda b, c: (0,) * len(shape))
    return pl.pallas_call(
        _ret_prompt_kernel,
        grid=(batch, nc),
        in_specs=[
            pl.BlockSpec((rows, W_RETBLK), lambda b, c: (b * nc + c, Z_RET // W_RETBLK)),
            pl.BlockSpec((dk, rows), lambda b, c: (0, b * nc + c)),
            pl.BlockSpec((3, rows, LANE), lambda b, c: (0, c, 0)),
            pl.BlockSpec((2, R_DK // 2, rows), lambda b, c: (0, 0, c)),
            const((R_HEADS, chunk, chunk)), const((chunk, R_HEADS * R_DV)),
            const((dk, chunk)), const((dk, R_DV)),
            pl.BlockSpec((1, dk, R_DV), lambda b, c: (b, 0, 0)),
        ],
        out_specs=[pl.BlockSpec((rows, R_HEADS * R_DV), lambda b, c: (b * nc + c, 0)),
                   pl.BlockSpec((1, dk, R_DV), lambda b, c: (b, 0, 0))],
        out_shape=[jax.ShapeDtypeStruct((batch * seq, R_HEADS * R_DV), BF16),
                   jax.ShapeDtypeStruct((batch, dk, R_DV), F32)],
        scratch_shapes=[pltpu.VMEM((dk, R_DV), F32)],
        compiler_params=_params(("parallel", "arbitrary")),
        name="ret_prompt",
    )(z, rkt, rtab, ctab, dmat, qdec, kdec, cdec, s0)


def _ret_sample_kernel(zr_ref, rkt_ref, rtab_ref, ctab_ref, dmat_ref, qdec_ref, kdec_ref, cdec_ref,
                       s0_ref, o_ref, s_ref, ocr_scr, *, dec_seq):
    q, kt, v, rg = _ret_prepare(zr_ref, rkt_ref, rtab_ref, ctab_ref)
    n = q.shape[0]
    nseq = n // dec_seq
    ktb = kt.astype(BF16)
    kd = kt * kdec_ref[...]
    vb = v.astype(BF16)
    qms = [jnp.where(_head_lane_mask(h, n), q, 0.0).astype(BF16) for h in range(R_HEADS)]
    tok = lax.broadcasted_iota(jnp.int32, (R_DK, n), 1) // dec_seq

    for b in range(nseq):
        r0 = b * dec_seq
        sb = s0_ref[b].astype(BF16)
        qstack = jnp.concatenate([qm[r0:r0 + dec_seq] for qm in qms], axis=0)
        res = _dot(qstack, sb)
        for h in range(R_HEADS):
            ocr_scr[r0:r0 + dec_seq, h * R_DV:(h + 1) * R_DV] = res[h * dec_seq:(h + 1) * dec_seq]

    for h in range(R_HEADS):
        vh = vb[:, h * R_DV:(h + 1) * R_DV]
        hs = slice(h * R_DV, (h + 1) * R_DV)
        sc = _dot(qms[h], ktb) * dmat_ref[h]
        o_h = _dot(sc.astype(BF16), vh) + ocr_scr[:, hs] * qdec_ref[:, hs]
        o_ref[:, hs] = _ret_finish(o_h, rg[:, hs])
        rows = slice(h * R_DK, (h + 1) * R_DK)
        kdh = kd[rows]
        for b in range(nseq):
            kb = jnp.where(tok == b, kdh, 0.0).astype(BF16)
            s_ref[b, rows, :] = s0_ref[b, rows, :] * cdec_ref[rows, :] + _dot(kb, vh)


def _ret_sample(z, rkt, rtab, ctab, consts, s0, dec_seq):
    t = z.shape[0]
    n = min(LANE, t)
    nseq = n // dec_seq
    dmat, qdec, kdec, cdec = consts
    dk = R_HEADS * R_DK
    const = lambda shape: pl.BlockSpec(shape, lambda s: (0,) * len(shape))
    return pl.pallas_call(
        functools.partial(_ret_sample_kernel, dec_seq=dec_seq),
        grid=(t // n,),
        in_specs=[
            pl.BlockSpec((n, W_RETBLK), lambda s: (s, Z_RET // W_RETBLK)),
            pl.BlockSpec((dk, n), lambda s: (0, s)),
            pl.BlockSpec((3, n, LANE), lambda s: (0, s, 0)),
            pl.BlockSpec((2, R_DK // 2, n), lambda s: (0, 0, s)),
            const((R_HEADS, n, n)), const((n, R_HEADS * R_DV)),
            const((dk, n)), const((dk, R_DV)),
            pl.BlockSpec((nseq, dk, R_DV), lambda s: (s, 0, 0)),
        ],
        out_specs=[pl.BlockSpec((n, R_HEADS * R_DV), lambda s: (s, 0)),
                   pl.BlockSpec((nseq, dk, R_DV), lambda s: (s, 0, 0))],
        out_shape=[jax.ShapeDtypeStruct((t, R_HEADS * R_DV), BF16),
                   jax.ShapeDtypeStruct((t // dec_seq, dk, R_DV), F32)],
        scratch_shapes=[pltpu.VMEM((n, R_HEADS * R_DV), F32)],
        compiler_params=_params(("parallel",)),
        name="ret_sample",
    )(z, rkt, rtab, ctab, dmat, qdec, kdec, cdec, s0)


def _merge_tail(y_conv, gate_ref, omla_ref, oret_ref, x_ref, wco_ref, wmo_ref, wro_ref, wo_ref, y_ref):
    br_a = _dot(y_conv.astype(BF16), wco_ref[...])
    br_b = _dot(omla_ref[...], wmo_ref[...])
    br_c = _dot(oret_ref[...], wro_ref[...])
    mixed = (jax.nn.sigmoid(gate_ref[:, :D_MODEL]) * br_a
             + jax.nn.sigmoid(gate_ref[:, D_MODEL:2 * D_MODEL]) * br_b
             + jax.nn.sigmoid(gate_ref[:, 2 * D_MODEL:]) * br_c)
    y_ref[...] = x_ref[...] + _dot(mixed.astype(BF16), wo_ref[...])


def _merge_prompt_kernel(zc_ref, gate_ref, omla_ref, oret_ref, x_ref, st_ref, wc_ref,
                         wco_ref, wmo_ref, wro_ref, wo_ref, y_ref, nb_ref, ext_scr):
    tm = zc_ref.shape[0]

    @pl.when(pl.program_id(1) == 0)
    def _():
        ext_scr[0:SUBLANE, :] = st_ref[0]

    cb = zc_ref[:, :CONV_DIM]
    u = zc_ref[:, CONV_DIM:2 * CONV_DIM] * zc_ref[:, 2 * CONV_DIM:]
    ext_scr[SUBLANE:SUBLANE + tm, :] = u
    conv = (ext_scr[SUBLANE - 2:SUBLANE - 2 + tm, :] * wc_ref[0:1, :]
            + ext_scr[SUBLANE - 1:SUBLANE - 1 + tm, :] * wc_ref[1:2, :]
            + u * wc_ref[2:3, :])
    tail = ext_scr[tm:tm + SUBLANE, :]
    ext_scr[0:SUBLANE, :] = tail
    nb_ref[0] = tail
    _merge_tail(cb * conv, gate_ref, omla_ref, oret_ref, x_ref, wco_ref, wmo_ref, wro_ref, wo_ref, y_ref)


def _merge_sample_kernel(zc_ref, gate_ref, omla_ref, oret_ref, x_ref, st0_ref, st1_ref, wc_ref,
                         wco_ref, wmo_ref, wro_ref, wo_ref, y_ref, u_ref, *, dec_seq):
    cb = zc_ref[:, :CONV_DIM]
    u = zc_ref[:, CONV_DIM:2 * CONV_DIM] * zc_ref[:, 2 * CONV_DIM:]
    u_ref[...] = u
    pos = lax.broadcasted_iota(jnp.int32, u.shape, 0) % dec_seq
    um1 = jnp.where(pos == 0, st1_ref[...], pltpu.roll(u, 1, 0))
    um2 = jnp.where(pos == 0, st0_ref[...], jnp.where(pos == 1, st1_ref[...], pltpu.roll(u, 2, 0)))
    conv = um2 * wc_ref[0:1, :] + um1 * wc_ref[1:2, :] + u * wc_ref[2:3, :]
    _merge_tail(cb * conv, gate_ref, omla_ref, oret_ref, x_ref, wco_ref, wmo_ref, wro_ref, wo_ref, y_ref)


def _merge_weight_specs(ngrid):
    const = lambda shape: pl.BlockSpec(shape, lambda *_: (0,) * len(shape))
    return [const((SUBLANE, CONV_DIM)), const((CONV_DIM, D_MODEL)), const((HEADS * VDIM, D_MODEL)),
            const((R_HEADS * R_DV, D_MODEL)), const((D_MODEL, D_MODEL))]


def _merge_prompt(z, omla, oret, x, st_pad, weights, batch, seq):
    tm = min(512, seq)
    nt = seq // tm
    row = lambda b, i: b * nt + i
    return pl.pallas_call(
        _merge_prompt_kernel,
        grid=(batch, nt),
        in_specs=[
            pl.BlockSpec((tm, W_CONVBLK), lambda b, i: (row(b, i), Z_CONV // W_CONVBLK)),
            pl.BlockSpec((tm, 3 * D_MODEL), lambda b, i: (row(b, i), Z_GATE // (3 * D_MODEL))),
            pl.BlockSpec((tm, HEADS * VDIM), lambda b, i: (row(b, i), 0)),
            pl.BlockSpec((tm, R_HEADS * R_DV), lambda b, i: (row(b, i), 0)),
            pl.BlockSpec((tm, D_MODEL), lambda b, i: (row(b, i), 0)),
            pl.BlockSpec((1, SUBLANE, CONV_DIM), lambda b, i: (b, 0, 0)),
        ] + _merge_weight_specs(2),
        out_specs=[pl.BlockSpec((tm, D_MODEL), lambda b, i: (row(b, i), 0)),
                   pl.BlockSpec((1, SUBLANE, CONV_DIM), lambda b, i: (b, 0, 0))],
        out_shape=[jax.ShapeDtypeStruct((batch * seq, D_MODEL), F32),
                   jax.ShapeDtypeStruct((batch, SUBLANE, CONV_DIM), F32)],
        scratch_shapes=[pltpu.VMEM((tm + SUBLANE, CONV_DIM), F32)],
        compiler_params=_params(("parallel", "arbitrary")),
        name="merge_prompt",
    )(z, z, omla, oret, x, st_pad, *weights)


def _merge_sample(z, omla, oret, x, st0, st1, weights, dec_seq):
    t = z.shape[0]
    tm = min(512, t)
    return pl.pallas_call(
        functools.partial(_merge_sample_kernel, dec_seq=dec_seq),
        grid=(t // tm,),
        in_specs=[
            pl.BlockSpec((tm, W_CONVBLK), lambda i: (i, Z_CONV // W_CONVBLK)),
            pl.BlockSpec((tm, 3 * D_MODEL), lambda i: (i, Z_GATE // (3 * D_MODEL))),
            pl.BlockSpec((tm, HEADS * VDIM), lambda i: (i, 0)),
            pl.BlockSpec((tm, R_HEADS * R_DV), lambda i: (i, 0)),
            pl.BlockSpec((tm, D_MODEL), lambda i: (i, 0)),
            pl.BlockSpec((tm, CONV_DIM), lambda i: (i, 0)),
            pl.BlockSpec((tm, CONV_DIM), lambda i: (i, 0)),
        ] + _merge_weight_specs(1),
        out_specs=[pl.BlockSpec((tm, D_MODEL), lambda i: (i, 0)),
                   pl.BlockSpec((tm, CONV_DIM), lambda i: (i, 0))],
        out_shape=[jax.ShapeDtypeStruct((t, D_MODEL), F32),
                   jax.ShapeDtypeStruct((t, CONV_DIM), F32)],
        compiler_params=_params(("parallel",)),
        name="merge_sample",
    )(z, z, omla, oret, x, st0, st1, *weights)


def _ffn_kernel(x_ref, g_ref, wu_ref, wd_ref, y_ref, h_scr, acc_scr):
    j = pl.program_id(1)

    @pl.when(j == 0)
    def _():
        x = x_ref[...]
        ms = jnp.mean(x * x, axis=-1, keepdims=True)
        h_scr[...] = (x * lax.rsqrt(ms + EPS) * g_ref[...]).astype(BF16)
        acc_scr[...] = x

    a = jnp.maximum(_dot(h_scr[...], wu_ref[...]), 0.0)
    acc_scr[...] += _dot((a * a).astype(BF16), wd_ref[...])

    @pl.when(j == pl.num_programs(1) - 1)
    def _():
        y_ref[...] = acc_scr[...]


def _ffn(x, g, wu, wd):
    t = x.shape[0]
    tm = min(1024, t)
    tf = 1024
    return pl.pallas_call(
        _ffn_kernel,
        grid=(t // tm, D_FF // tf),
        in_specs=[
            pl.BlockSpec((tm, D_MODEL), lambda i, j: (i, 0)),
            pl.BlockSpec((1, D_MODEL), lambda i, j: (0, 0)),
            pl.BlockSpec((D_MODEL, tf), lambda i, j: (0, j)),
            pl.BlockSpec((tf, D_MODEL), lambda i, j: (j, 0)),
        ],
        out_specs=pl.BlockSpec((tm, D_MODEL), lambda i, j: (i, 0)),
        out_shape=jax.ShapeDtypeStruct((t, D_MODEL), F32),
        scratch_shapes=[pltpu.VMEM((tm, D_MODEL), BF16), pltpu.VMEM((tm, D_MODEL), F32)],
        compiler_params=_params(("parallel", "arbitrary")),
        name="ffn",
    )(x, g, wu, wd)


def _rope_tables(pos):
    posf = pos.astype(F32)[:, None]
    n = pos.shape[0]

    def cs(half):
        inv = THETA ** (-jnp.arange(half, dtype=F32) / half)
        ang = posf * inv[None, :]
        return jnp.cos(ang), jnp.sin(ang)

    c16, s16 = cs(ROPE // 2)
    one = lambda w: jnp.ones((n, w), F32)
    zero = lambda w: jnp.zeros((n, w), F32)
    qtab = jnp.stack([
        jnp.concatenate([one(NOPE), c16, c16, one(HT - QK)], axis=1),
        jnp.concatenate([zero(NOPE), -s16, zero(HT - NOPE - ROPE // 2)], axis=1),
        jnp.concatenate([zero(NOPE + ROPE // 2), s16, zero(HT - QK)], axis=1)])
    ktab = jnp.stack([
        jnp.concatenate([c16, c16, one(LANE - ROPE)], axis=1),
        jnp.concatenate([-s16, zero(LANE - ROPE // 2)], axis=1),
        jnp.concatenate([zero(ROPE // 2), s16, zero(LANE - ROPE)], axis=1)])
    c32, s32 = cs(R_DK // 2)
    z32 = zero(R_DK // 2)
    rtab = jnp.stack([
        jnp.concatenate([c32] * (LANE // (R_DK // 2)), axis=1),
        jnp.concatenate([-s32, z32] * (LANE // R_DK), axis=1),
        jnp.concatenate([z32, s32] * (LANE // R_DK), axis=1)])
    ctab = jnp.stack([c32.T, s32.T])
    return qtab, ktab, rtab, ctab


def _ret_consts(n, chunk):
    log_g = jnp.log1p(-(2.0 ** (-5.0 - jnp.arange(R_HEADS, dtype=F32))))
    idx = jnp.arange(n)
    loc = (idx % chunk).astype(F32)
    same = (idx[:, None] // chunk) == (idx[None, :] // chunk)
    diff = loc[:, None] - loc[None, :]
    dmat = jnp.where(same & (diff >= 0), jnp.exp(log_g[:, None, None] * jnp.maximum(diff, 0.0)), 0.0)
    qd = jnp.exp(log_g[None, :] * (loc[:, None] + 1.0))
    qdec = jnp.repeat(qd, R_DV, axis=1)
    kd = jnp.exp(log_g[:, None] * (chunk - 1.0 - loc[None, :]))
    kdec = jnp.repeat(kd, R_DK, axis=0)
    cdec = jnp.repeat(jnp.exp(log_g * chunk)[:, None], R_DK, axis=0) * jnp.ones((1, R_DV), F32)
    return dmat.astype(F32), qdec, kdec, cdec


def _layer_weights(l, g_mix, w_in, w_conv, w_conv_out, g_q_lat, w_uq, g_kv_lat, w_ukv, g_qn, g_kn,
                   w_mla_out, w_ret_out, w_o, g_ffn, w_up, w_down):
    wi = w_in[l]
    o_cq, o_rq, o_rk, o_gl, o_end = 1536, 2208, 2464, 3744, 6816
    w_perm = jnp.concatenate(
        [wi[:, :o_cq], wi[:, o_rq:o_gl], wi[:, o_gl:o_end], wi[:, o_cq:o_rq],
         jnp.zeros((D_MODEL, LANE - ROPE), F32)], axis=1).astype(BF16)
    w_rk_t = wi[:, o_rk:o_rk + R_HEADS * R_DK].T.astype(BF16)
    wuq = jnp.pad(w_uq[l].reshape(Q_LORA, HEADS, QK), ((0, 0), (0, 0), (0, HT - QK)))
    wuq = wuq.reshape(Q_LORA, HEADS * HT).astype(BF16)
    wkv = w_ukv[l].reshape(KV_LORA, HEADS, NOPE + VDIM)
    w_uk, w_uv = wkv[..., :NOPE], wkv[..., NOPE:]
    wk_pad = jnp.pad(w_uk, ((0, 0), (0, 0), (0, HT - NOPE))).reshape(KV_LORA, HEADS * HT).astype(BF16)
    wv = w_uv.reshape(KV_LORA, HEADS * VDIM).astype(BF16)
    wv_pad = jnp.pad(w_uv, ((0, 0), (0, 0), (0, HT - VDIM))).reshape(KV_LORA, HEADS * HT).T.astype(BF16)
    wukt = w_uk.reshape(KV_LORA, HEADS * NOPE).T.astype(BF16)
    j = np.arange(ROPE)
    pm = np.zeros((LANE, HEADS * HT), np.float32)
    for h in range(HEADS):
        pm[j, h * HT + NOPE + j] = 1.0
    shift = np.zeros((HT, LANE), np.float32)
    shift[NOPE + j, j] = 1.0
    wabs = jnp.pad(jnp.transpose(w_uk, (1, 2, 0)), ((0, 0), (0, HT - NOPE), (0, 0)))
    wcomb = jnp.concatenate([wabs, jnp.broadcast_to(jnp.asarray(shift), (HEADS, HT, LANE))],
                            axis=2).astype(BF16)
    pad_gain = lambda g: jnp.pad(g, (0, HT - QK))[None, :]
    return dict(
        g_mix=g_mix[l][None, :], w_perm=w_perm, w_rk_t=w_rk_t,
        g_q_lat=g_q_lat[l][None, :], wuq=wuq, g_kv_lat=g_kv_lat[l][None, :],
        gqn=pad_gain(g_qn[l]) * (QK ** -0.5), gkn=pad_gain(g_kn[l]),
        wk_pad=wk_pad, pmat=jnp.asarray(pm, BF16), wv=wv, wv_pad=wv_pad, wukt=wukt, wcomb=wcomb,
        merge=(jnp.pad(w_conv[l], ((0, SUBLANE - CONV_W), (0, 0))), w_conv_out[l].astype(BF16),
               w_mla_out[l].astype(BF16), w_ret_out[l].astype(BF16), w_o[l].astype(BF16)),
        g_ffn=g_ffn[l][None, :], w_up=w_up[l].astype(BF16), w_down=w_down[l].astype(BF16))


def kernel(x_prompt, x_sample, cache_ckv, cache_kpe, state_conv, state_ret, page_table, g_mix, w_in,
           w_conv, w_conv_out, g_q_lat, w_uq, g_kv_lat, w_ukv, g_qn, g_kn, w_mla_out, w_ret_out, w_o,
           g_ffn, w_up, w_down):
    batch, seq, _ = x_prompt.shape
    nb, dec_seq, _ = x_sample.shape
    depth = w_in.shape[0]
    past = page_table.shape[1] * PAGE
    assert dec_seq == SUBLANE
    tp, ts = batch * seq, nb * dec_seq
    dk = R_HEADS * R_DK

    tabs_p = _rope_tables(jnp.arange(seq))
    tabs_s = _rope_tables(jnp.tile(past + jnp.arange(dec_seq), nb))
    chunk_p = math.gcd(seq, R_CHUNK)
    consts_p = _ret_consts(chunk_p, chunk_p)
    consts_s = _ret_consts(min(LANE, ts), dec_seq)
    tm_p, tm_s = min(512, seq), min(512, ts)

    cache_kpe_t = jnp.swapaxes(cache_kpe, 2, 3)
    yp = x_prompt.reshape(tp, D_MODEL)
    ys = x_sample.reshape(ts, D_MODEL)
    conv0 = jnp.zeros((batch, SUBLANE, CONV_DIM), F32)
    ret0 = jnp.zeros((batch, dk, R_DV), F32)
    outs = {k: [] for k in ("ckv_p", "kpe_p", "conv_p", "ret_p", "ckv_s", "kpe_s", "conv_s", "ret_s")}

    for l in range(depth):
        w = _layer_weights(l, g_mix, w_in, w_conv, w_conv_out, g_q_lat, w_uq, g_kv_lat, w_ukv, g_qn,
                           g_kn, w_mla_out, w_ret_out, w_o, g_ffn, w_up, w_down)

        qtab, ktab, rtab, ctab = tabs_p
        z, rkt = _inproj(yp, w["g_mix"], w["w_perm"], w["w_rk_t"])
        q, ckvn, kper, k, v = _mlaprep(z, qtab, ktab, tm_p, w["g_q_lat"], w["wuq"], w["g_kv_lat"],
                                       w["gqn"], (w["wk_pad"], w["pmat"], w["wv_pad"], w["gkn"]))
        omla = _flash(q, k, v, batch, seq)
        oret, snew = _ret_prompt(z, rkt, rtab, ctab, consts_p, ret0, batch, seq)
        ymid, nbuf = _merge_prompt(z, omla, oret, yp, conv0, w["merge"], batch, seq)
        yp = _ffn(ymid, w["g_ffn"], w["w_up"], w["w_down"])
        outs["ckv_p"].append(ckvn.reshape(batch, seq, KV_LORA))
        outs["kpe_p"].append(kper.reshape(batch, seq, ROPE))
        outs["conv_p"].append(nbuf[:, SUBLANE - (CONV_W - 1):, :])
        outs["ret_p"].append(snew.reshape(batch, R_HEADS, R_DK, R_DV))

        qtab, ktab, rtab, ctab = tabs_s
        z, rkt = _inproj(ys, w["g_mix"], w["w_perm"], w["w_rk_t"])
        q, ckvn, kper = _mlaprep(z, qtab, ktab, tm_s, w["g_q_lat"], w["wuq"], w["g_kv_lat"],
                                 w["gqn"], None)
        qa, qp = _qabs(q, w["gkn"], w["wcomb"], nb, dec_seq)
        cnew = jnp.pad(ckvn.reshape(nb, dec_seq, KV_LORA), ((0, 0), (0, PAGE - dec_seq), (0, 0)))
        pnew_t = jnp.pad(jnp.swapaxes(kper.reshape(nb, dec_seq, ROPE), 1, 2),
                         ((0, 0), (0, 0), (0, PAGE - dec_seq)))
        omla = _decode(page_table, l, cache_ckv, cache_kpe_t,
                       qa.reshape(nb, HEADS * dec_seq, KV_LORA), qp.reshape(nb, HEADS * dec_seq, LANE),
                       w["wukt"], w["wv"], cnew, pnew_t).reshape(ts, HEADS * VDIM)
        oret, snew = _ret_sample(z, rkt, rtab, ctab, consts_s, state_ret[l].reshape(nb, dk, R_DV), dec_seq)
        st = state_conv[l]
        st0 = jnp.repeat(st[:, 0, :], dec_seq, axis=0)
        st1 = jnp.repeat(st[:, 1, :], dec_seq, axis=0)
        ymid, u = _merge_sample(z, omla, oret, ys, st0, st1, w["merge"], dec_seq)
        ys = _ffn(ymid, w["g_ffn"], w["w_up"], w["w_down"])
        outs["ckv_s"].append(ckvn.reshape(nb, dec_seq, KV_LORA))
        outs["kpe_s"].append(kper.reshape(nb, dec_seq, ROPE))
        outs["conv_s"].append(u.reshape(nb, dec_seq, CONV_DIM)[:, dec_seq - (CONV_W - 1):, :])
        outs["ret_s"].append(snew.reshape(nb, R_HEADS, R_DK, R_DV))

    return (yp.reshape(batch, seq, D_MODEL), ys.reshape(nb, dec_seq, D_MODEL),
            jnp.stack(outs["ckv_p"]), jnp.stack(outs["kpe_p"]), jnp.stack(outs["conv_p"]),
            jnp.stack(outs["ret_p"]),
            jnp.stack(outs["ckv_s"]), jnp.stack(outs["kpe_s"]), jnp.stack(outs["conv_s"]),
            jnp.stack(outs["ret_s"]))
```

```python
import functools
import math

import jax
import jax.numpy as jnp
import numpy as np
from jax import lax
from jax.experimental import pallas as pl
from jax.experimental.pallas import tpu as pltpu

F32 = jnp.float32
BF16 = jnp.bfloat16

D_MODEL = 1024
PAGE = 128
CONV_DIM = 512
CONV_W = 3
HEADS = 8
NOPE = 64
ROPE = 32
QK = NOPE + ROPE
VDIM = 64
Q_LORA = 384
KV_LORA = 256
R_HEADS = 4
R_DK = 64
R_DV = 128
R_CHUNK = 128
D_FF = 4096
THETA = 10000.0
EPS = 1e-6
NEG = -1e30

LANE = 128
SUBLANE = 8
HT = LANE

Z_CONV = 0
Z_RET = 1536
Z_GATE = 3072
Z_MLA = 6144
Z_W = 6912
W_CONVBLK = 3 * CONV_DIM
W_RETBLK = 2 * R_HEADS * R_DK + 2 * R_HEADS * R_DV
W_MLABLK = Q_LORA + KV_LORA + LANE

VMEM_LIMIT = 56 * 1024 * 1024


def _dot(a, b):
    return jnp.dot(a, b, preferred_element_type=F32)


def _dot_nt(a, b):
    return lax.dot_general(a, b, (((1,), (1,)), ((), ())), preferred_element_type=F32)


def _params(sem):
    return pltpu.CompilerParams(dimension_semantics=sem, vmem_limit_bytes=VMEM_LIMIT)


def _rope_rows(x, tab_ref, half):
    return (x * tab_ref[0] + pltpu.roll(x, LANE - half, 1) * tab_ref[1]
            + pltpu.roll(x, half, 1) * tab_ref[2])


def _inproj_kernel(x_ref, g_ref, w_ref, wrk_ref, z_ref, rkt_ref, h_scr):
    @pl.when(pl.program_id(1) == 0)
    def _():
        x = x_ref[...]
        ms = jnp.mean(x * x, axis=-1, keepdims=True)
        hb = (x * lax.rsqrt(ms + EPS) * g_ref[...]).astype(BF16)
        h_scr[...] = hb
        rkt_ref[...] = _dot_nt(wrk_ref[...], hb)

    z_ref[...] = _dot(h_scr[...], w_ref[...])


def _inproj(x, g, w_perm, w_rk_t):
    t = x.shape[0]
    tm = min(1024, t)
    tn = 2304
    return pl.pallas_call(
        _inproj_kernel,
        grid=(t // tm, Z_W // tn),
        in_specs=[
            pl.BlockSpec((tm, D_MODEL), lambda i, j: (i, 0)),
            pl.BlockSpec((1, D_MODEL), lambda i, j: (0, 0)),
            pl.BlockSpec((D_MODEL, tn), lambda i, j: (0, j)),
            pl.BlockSpec((R_HEADS * R_DK, D_MODEL), lambda i, j: (0, 0)),
        ],
        out_specs=[
            pl.BlockSpec((tm, tn), lambda i, j: (i, j)),
            pl.BlockSpec((R_HEADS * R_DK, tm), lambda i, j: (0, i)),
        ],
        out_shape=[jax.ShapeDtypeStruct((t, Z_W), F32),
                   jax.ShapeDtypeStruct((R_HEADS * R_DK, t), F32)],
        scratch_shapes=[pltpu.VMEM((tm, D_MODEL), BF16)],
        compiler_params=_params(("parallel", "arbitrary")),
        name="inproj",
    )(x, g, w_perm, w_rk_t)


def _mlaprep_kernel(z_ref, qtab_ref, ktab_ref, gq_ref, wuq_ref, gkv_ref, gqn_ref,
                    *rest, expand):
    if expand:
        wk_ref, pmat_ref, wv_ref, gkn_ref, q_ref, ckv_ref, kpe_ref, k_ref, v_ref = rest
    else:
        q_ref, ckv_ref, kpe_ref = rest
    cq = z_ref[:, :Q_LORA]
    ckv = z_ref[:, Q_LORA:Q_LORA + KV_LORA]
    kpe = z_ref[:, Q_LORA + KV_LORA:]

    cqn = cq * lax.rsqrt(jnp.mean(cq * cq, axis=-1, keepdims=True) + EPS) * gq_ref[...]
    qf = _dot(cqn.astype(BF16), wuq_ref[...])
    for h in range(HEADS):
        xr = _rope_rows(qf[:, h * HT:(h + 1) * HT], qtab_ref, ROPE // 2)
        ms = jnp.sum(xr * xr, axis=-1, keepdims=True) * (1.0 / QK)
        q_ref[:, h * HT:(h + 1) * HT] = (xr * lax.rsqrt(ms + EPS) * gqn_ref[...]).astype(BF16)

    ckvn = ckv * lax.rsqrt(jnp.mean(ckv * ckv, axis=-1, keepdims=True) + EPS) * gkv_ref[...]
    ckv_ref[...] = ckvn
    kper = _rope_rows(kpe, ktab_ref, ROPE // 2)
    kpe_ref[...] = kper[:, :ROPE]

    if expand:
        cb = ckvn.astype(BF16)
        p_hi = kper.astype(BF16)
        p_lo = (kper - p_hi.astype(F32)).astype(BF16)
        kf = _dot(cb, wk_ref[...]) + _dot(p_hi, pmat_ref[...]) + _dot(p_lo, pmat_ref[...])
        for h in range(HEADS):
            xk = kf[:, h * HT:(h + 1) * HT]
            ms = jnp.sum(xk * xk, axis=-1, keepdims=True) * (1.0 / QK)
            k_ref[:, h * HT:(h + 1) * HT] = (xk * lax.rsqrt(ms + EPS) * gkn_ref[...]).astype(BF16)
        vt = _dot_nt(wv_ref[...], cb)
        row = lax.broadcasted_iota(jnp.int32, vt.shape, 0) % HT
        v_ref[...] = jnp.where(row == VDIM, 1.0, vt).astype(BF16)


def _mlaprep(z, qtab, ktab, tm, gq, wuq, gkv, gqn, expand_args):
    t = z.shape[0]
    n_seq_tiles = qtab.shape[1] // tm
    expand = expand_args is not None
    const = lambda shape: pl.BlockSpec(shape, lambda i: (0,) * len(shape))
    tab = pl.BlockSpec((3, tm, LANE), lambda i: (0, i % n_seq_tiles, 0))
    in_specs = [
        pl.BlockSpec((tm, W_MLABLK), lambda i: (i, Z_MLA // W_MLABLK)),
        tab, tab,
        const((1, Q_LORA)), const((Q_LORA, HEADS * HT)), const((1, KV_LORA)), const((1, HT)),
    ]
    args = [z, qtab, ktab, gq, wuq, gkv, gqn]
    out_specs = [pl.BlockSpec((tm, HEADS * HT), lambda i: (i, 0)),
                 pl.BlockSpec((tm, KV_LORA), lambda i: (i, 0)),
                 pl.BlockSpec((tm, ROPE), lambda i: (i, 0))]
    out_shape = [jax.ShapeDtypeStruct((t, HEADS * HT), BF16),
                 jax.ShapeDtypeStruct((t, KV_LORA), F32),
                 jax.ShapeDtypeStruct((t, ROPE), F32)]
    if expand:
        wk, pmat, wv, gkn = expand_args
        in_specs += [const((KV_LORA, HEADS * HT)), const((LANE, HEADS * HT)),
                     const((HEADS * HT, KV_LORA)), const((1, HT))]
        args += [wk, pmat, wv, gkn]
        out_specs += [pl.BlockSpec((tm, HEADS * HT), lambda i: (i, 0)),
                      pl.BlockSpec((HEADS * HT, tm), lambda i: (0, i))]
        out_shape += [jax.ShapeDtypeStruct((t, HEADS * HT), BF16),
                      jax.ShapeDtypeStruct((HEADS * HT, t), BF16)]
    return pl.pallas_call(
        functools.partial(_mlaprep_kernel, expand=expand),
        grid=(t // tm,),
        in_specs=in_specs, out_specs=out_specs, out_shape=out_shape,
        compiler_params=_params(("parallel",)),
        name="mlaprep_expand" if expand else "mlaprep",
    )(*args)


def _flash_kernel(q_ref, k_ref, vt_ref, o_ref, sa_scr, sb_scr, m_scr, acc_scr, *, tq, tk):
    qi = pl.program_id(2)
    m_scr[...] = jnp.full(m_scr.shape, NEG, F32)
    acc_scr[...] = jnp.zeros(acc_scr.shape, F32)

    def scores(ki, s_scr):
        start = pl.multiple_of(ki * tk, tk)
        for h in range(2):
            hs = slice(h * HT, (h + 1) * HT)
            s_scr[h] = _dot_nt(k_ref[pl.ds(start, tk), hs], q_ref[:, hs])

    def attend(ki, s_scr, diag):
        start = pl.multiple_of(ki * tk, tk)
        for h in range(2):
            s = s_scr[h]
            if diag is not None:
                key = lax.broadcasted_iota(jnp.int32, (tk, tq), 0) + diag * tk
                qry = lax.broadcasted_iota(jnp.int32, (tk, tq), 1)
                s = jnp.where(key <= qry, s, NEG)
            m = m_scr[h]
            m_new = jnp.maximum(m, jnp.max(s, axis=0, keepdims=True))
            p = jnp.exp2(s - m_new).astype(BF16)
            vt = vt_ref[h * HT:(h + 1) * HT, pl.ds(start, tk)]
            acc_scr[h] = jnp.exp2(m - m_new) * acc_scr[h] + _dot(vt, p)
            m_scr[h] = m_new

    def body(j, carry):
        scores(2 * j + 1, sb_scr)
        attend(2 * j, sa_scr, None)
        scores(2 * j + 2, sa_scr)
        attend(2 * j + 1, sb_scr, None)
        return carry

    scores(0, sa_scr)
    lax.fori_loop(0, qi, body, 0)
    scores(2 * qi + 1, sb_scr)
    attend(2 * qi, sa_scr, 0)
    attend(2 * qi + 1, sb_scr, 1)
    outs = []
    for h in range(2):
        a = acc_scr[h]
        outs.append((a / a[VDIM:VDIM + 1, :]).T)
    lane = lax.broadcasted_iota(jnp.int32, (tq, LANE), 1)
    o_ref[...] = jnp.where(lane < VDIM, outs[0], pltpu.roll(outs[1], VDIM, 1)).astype(BF16)


def _flash(q, k, vt, batch, seq):
    tq = min(512, seq)
    tk = tq // 2
    nq = seq // tq
    return pl.pallas_call(
        functools.partial(_flash_kernel, tq=tq, tk=tk),
        grid=(batch, HEADS // 2, nq),
        in_specs=[
            pl.BlockSpec((tq, 2 * HT), lambda b, hp, i: (b * nq + i, hp)),
            pl.BlockSpec((seq, 2 * HT), lambda b, hp, i: (b, hp)),
            pl.BlockSpec((2 * HT, seq), lambda b, hp, i: (hp, b)),
        ],
        out_specs=pl.BlockSpec((tq, 2 * VDIM), lambda b, hp, i: (b * nq + i, hp)),
        out_shape=jax.ShapeDtypeStruct((batch * seq, HEADS * VDIM), BF16),
        scratch_shapes=[pltpu.VMEM((2, tk, tq), F32), pltpu.VMEM((2, tk, tq), F32),
                        pltpu.VMEM((2, 1, tq), F32), pltpu.VMEM((2, HT, tq), F32)],
        compiler_params=_params(("parallel", "parallel", "arbitrary")),
        name="flash",
    )(q, k, vt)


def _qabs_kernel(q_ref, gkn_ref, wcomb_ref, qa_ref, qp_ref):
    nb = qa_ref.shape[0]
    for h in range(HEADS):
        qg = (q_ref[:, h * HT:(h + 1) * HT].astype(F32) * gkn_ref[...]).astype(BF16)
        r = _dot(qg, wcomb_ref[h])
        qa_ref[:, h] = r[:, :KV_LORA].reshape(nb, SUBLANE, KV_LORA)
        qp_ref[:, h] = r[:, KV_LORA:].reshape(nb, SUBLANE, LANE)


def _qabs(q, gkn, wcomb, nb, dec_seq):
    t = q.shape[0]
    return pl.pallas_call(
        _qabs_kernel,
        grid=(1,),
        in_specs=[pl.BlockSpec((t, HEADS * HT), lambda i: (0, 0)),
                  pl.BlockSpec((1, HT), lambda i: (0, 0)),
                  pl.BlockSpec((HEADS, HT, KV_LORA + LANE), lambda i: (0, 0, 0))],
        out_specs=[pl.BlockSpec((nb, HEADS, dec_seq, KV_LORA), lambda i: (0, 0, 0, 0)),
                   pl.BlockSpec((nb, HEADS, dec_seq, LANE), lambda i: (0, 0, 0, 0))],
        out_shape=[jax.ShapeDtypeStruct((nb, HEADS, dec_seq, KV_LORA), F32),
                   jax.ShapeDtypeStruct((nb, HEADS, dec_seq, LANE), F32)],
        compiler_params=_params(("arbitrary",)),
        name="qabs",
    )(q, gkn, wcomb)


DEC_ROWS = HEADS * SUBLANE
DEC_KROWS = 3 * ROPE + ROPE
DEC_PROWS = DEC_ROWS + 16


def _decode_kernel(pt_ref, qa_ref, qp_ref, wukt_ref, wv_ref, cnew_ref, pnew_ref, ckv_hbm, kpe_hbm,
                   o_ref, cbuf, pbuf, sem, cbf, kt, wall, lpe, s_scr, m_scr, l_scr, acc_scr,
                   *, npg, sub, grp, layer):
    b = pl.program_id(0)
    g = pl.program_id(1)
    ng = pl.num_programs(1)
    step = b * ng + g
    slot = step % 2
    nk = HEADS * NOPE

    def page_copies(bb, gg, sl):
        copies = []
        for i in range(npg):
            pid = 0 if bb is None else pt_ref[bb, gg * npg + i]
            copies.append(pltpu.make_async_copy(ckv_hbm.at[layer, pid], cbuf.at[sl, i], sem.at[0, sl]))
            copies.append(pltpu.make_async_copy(kpe_hbm.at[layer, pid], pbuf.at[sl, i], sem.at[1, sl]))
        return copies

    @pl.when(step == 0)
    def _():
        for cp in page_copies(b, g, slot):
            cp.start()

    wrap = g + 1 == ng
    b_next = jnp.where(wrap, jnp.where(b + 1 == pl.num_programs(0), 0, b + 1), b)
    prefetch = page_copies(b_next, jnp.where(wrap, 0, g + 1), 1 - slot)
    for n, cp in enumerate(prefetch):
        cp.start(priority=n % 2)

    for cp in page_copies(None, None, slot):
        cp.wait()
    c_pages = [cbuf.at[slot, i] for i in range(npg)]
    p_pages = [pbuf.at[slot, i] for i in range(npg)]

    @pl.when(g == 0)
    def _():
        m_scr[...] = jnp.full(m_scr.shape, NEG, F32)
        l_scr[...] = jnp.zeros(l_scr.shape, F32)
        acc_scr[...] = jnp.zeros(acc_scr.shape, F32)
        kt[3 * ROPE:, :] = jnp.zeros((DEC_KROWS - 3 * ROPE, kt.shape[1]), BF16)
        wall[:nk, :] = wukt_ref[...]
        wall[nk:, :] = qa_ref[0].astype(BF16)
        lane = lax.broadcasted_iota(jnp.int32, (DEC_PROWS - DEC_ROWS, DEC_KROWS), 1)
        lpe[:DEC_ROWS, :] = qp_ref[0].astype(BF16)
        lpe[DEC_ROWS:, :] = jnp.where((lane >= ROPE) & (lane < 3 * ROPE), 1.0, 0.0).astype(BF16)

    def stage(i, c, kp_t):
        cols = slice(i * PAGE, (i + 1) * PAGE)
        cbf[cols, :] = c.astype(BF16)
        kt[:ROPE, cols] = kp_t.astype(BF16)
        kp2 = kp_t * kp_t
        hi = kp2.astype(BF16)
        kt[ROPE:2 * ROPE, cols] = hi
        kt[2 * ROPE:3 * ROPE, cols] = (kp2 - hi.astype(F32)).astype(BF16)

    def scores(start, width):
        ct = cbf[start:start + width, :]
        big = _dot_nt(wall[...], ct)
        ext = _dot(lpe[...], kt[:, start:start + width])
        kpsq = ext[DEC_ROWS:DEC_ROWS + 1]
        for h in range(HEADS):
            blk = big[h * NOPE:(h + 1) * NOPE]
            nsq = jnp.sum(blk * blk, axis=0, keepdims=True)
            rs = lax.rsqrt((nsq + kpsq) * (1.0 / QK) + EPS)
            hr = slice(h * SUBLANE, (h + 1) * SUBLANE)
            s_scr[hr, start:start + width] = (big[nk + h * SUBLANE:nk + (h + 1) * SUBLANE] + ext[hr]) * rs

    def update(width, mask):
        if mask is not None:
            s_scr[:, :width] = jnp.where(mask, s_scr[:, :width], NEG)
        m = m_scr[...]
        m_new = jnp.maximum(m, jnp.max(s_scr[:, :width], axis=-1, keepdims=True))
        alpha = jnp.exp2(m - m_new)
        acc = alpha * acc_scr[...]
        lsum = alpha * l_scr[...]
        chunk = min(grp, width)
        for c in range(width // chunk):
            cs = slice(c * chunk, (c + 1) * chunk)
            p = jnp.exp2(s_scr[:, cs] - m_new)
            lsum = lsum + jnp.sum(p, axis=-1, keepdims=True)
            acc = acc + _dot(p.astype(BF16), cbf[cs, :])
        l_scr[...] = lsum
        acc_scr[...] = acc
        m_scr[...] = m_new

    pages_per_sub = sub // PAGE
    for j in range(npg // pages_per_sub):
        for i in range(j * pages_per_sub, (j + 1) * pages_per_sub):
            stage(i, c_pages[i][...], p_pages[i][...])
        scores(j * sub, sub)
    update(npg * PAGE, None)

    @pl.when(g == ng - 1)
    def _():
        stage(0, cnew_ref[0], pnew_ref[0])
        scores(0, PAGE)
        r = lax.broadcasted_iota(jnp.int32, (DEC_ROWS, PAGE), 0)
        t = lax.broadcasted_iota(jnp.int32, (DEC_ROWS, PAGE), 1)
        update(PAGE, t <= (r % SUBLANE))
        olat = (acc_scr[...] / l_scr[...]).astype(BF16)
        full = _dot(olat, wv_ref[...])
        colh = lax.broadcasted_iota(jnp.int32, (SUBLANE, HEADS * VDIM), 1) // VDIM
        out = jnp.zeros((SUBLANE, HEADS * VDIM), F32)
        for h in range(HEADS):
            out = out + jnp.where(colh == h, full[h * SUBLANE:(h + 1) * SUBLANE], 0.0)
        o_ref[0] = out.astype(BF16)

    @pl.when(step == pl.num_programs(0) * ng - 1)
    def _():
        for cp in prefetch:
            cp.wait()


def _decode(page_table, layer, cache_ckv, cache_kpe_t, qa, qp, wukt, wv, cnew, pnew_t):
    nb, n_pages = page_table.shape
    npg = min(64, n_pages)
    sub = min(256, npg * PAGE)
    ngrp = n_pages // npg

    in_specs = [
        pl.BlockSpec((1, DEC_ROWS, KV_LORA), lambda b, g, pt: (b, 0, 0)),
        pl.BlockSpec((1, DEC_ROWS, DEC_KROWS), lambda b, g, pt: (b, 0, 0)),
        pl.BlockSpec((HEADS * NOPE, KV_LORA), lambda b, g, pt: (0, 0)),
        pl.BlockSpec((KV_LORA, HEADS * VDIM), lambda b, g, pt: (0, 0)),
        pl.BlockSpec((1, PAGE, KV_LORA), lambda b, g, pt: (b, 0, 0)),
        pl.BlockSpec((1, ROPE, PAGE), lambda b, g, pt: (b, 0, 0)),
        pl.BlockSpec(memory_space=pl.ANY),
        pl.BlockSpec(memory_space=pl.ANY),
    ]
    grid_spec = pltpu.PrefetchScalarGridSpec(
        num_scalar_prefetch=1,
        grid=(nb, ngrp),
        in_specs=in_specs,
        out_specs=pl.BlockSpec((1, SUBLANE, HEADS * VDIM), lambda b, g, pt: (b, 0, 0)),
        scratch_shapes=[
            pltpu.VMEM((2, npg, PAGE, KV_LORA), F32),
            pltpu.VMEM((2, npg, ROPE, PAGE), F32),
            pltpu.SemaphoreType.DMA((2, 2)),
            pltpu.VMEM((npg * PAGE, KV_LORA), BF16),
            pltpu.VMEM((DEC_KROWS, npg * PAGE), BF16),
            pltpu.VMEM((HEADS * NOPE + DEC_ROWS, KV_LORA), BF16),
            pltpu.VMEM((DEC_PROWS, DEC_KROWS), BF16),
            pltpu.VMEM((DEC_ROWS, npg * PAGE), F32),
            pltpu.VMEM((DEC_ROWS, 1), F32),
            pltpu.VMEM((DEC_ROWS, 1), F32),
            pltpu.VMEM((DEC_ROWS, KV_LORA), F32),
        ],
    )
    return pl.pallas_call(
        functools.partial(_decode_kernel, npg=npg, sub=sub, grp=min(512, npg * PAGE), layer=layer),
        grid_spec=grid_spec,
        out_shape=jax.ShapeDtypeStruct((nb, SUBLANE, HEADS * VDIM), BF16),
        compiler_params=_params(("arbitrary", "arbitrary")),
        name="decode",
    )(page_table, qa, qp, wukt, wv, cnew, pnew_t, cache_ckv, cache_kpe_t)


def _ret_prepare(zr_ref, rkt_ref, rtab_ref, ctab_ref):
    dk = R_HEADS * R_DK
    q = jnp.concatenate(
        [_rope_rows(zr_ref[:, i * LANE:(i + 1) * LANE], rtab_ref, R_DK // 2) for i in range(dk // LANE)],
        axis=1)
    cos_t = ctab_ref[0]
    sin_t = ctab_ref[1]
    half = R_DK // 2
    kparts = []
    for h in range(R_HEADS):
        a = rkt_ref[h * R_DK:h * R_DK + half, :]
        b = rkt_ref[h * R_DK + half:(h + 1) * R_DK, :]
        kparts += [a * cos_t - b * sin_t, b * cos_t + a * sin_t]
    kt = jnp.concatenate(kparts, axis=0) * (R_DK ** -0.5)
    v = zr_ref[:, 2 * dk:2 * dk + R_HEADS * R_DV]
    rg = zr_ref[:, 2 * dk + R_HEADS * R_DV:]
    return q, kt, v, rg


def _ret_finish(o_h, rg_h):
    ms = jnp.mean(o_h * o_h, axis=-1, keepdims=True)
    return (jax.nn.silu(rg_h) * (o_h * lax.rsqrt(ms + EPS))).astype(BF16)


def _head_lane_mask(h, n):
    lane = lax.broadcasted_iota(jnp.int32, (n, R_HEADS * R_DK), 1)
    return (lane // R_DK) == h


def _ret_prompt_kernel(zr_ref, rkt_ref, rtab_ref, ctab_ref, dmat_ref, qdec_ref, kdec_ref, cdec_ref,
                       s0_ref, o_ref, s_ref, s_scr):
    c = pl.program_id(1)

    @pl.when(c == 0)
    def _():
        s_scr[...] = s0_ref[0]

    q, kt, v, rg = _ret_prepare(zr_ref, rkt_ref, rtab_ref, ctab_ref)
    chunk = dmat_ref.shape[1]
    nchunk = q.shape[0] // chunk
    vb = v.astype(BF16)
    qms = [jnp.where(_head_lane_mask(h, q.shape[0]), q, 0.0).astype(BF16) for h in range(R_HEADS)]
    states = [s_scr[...]]
    o_in = []
    for c in range(nchunk):
        tok = slice(c * chunk, (c + 1) * chunk)
        ktc = kt[:, tok]
        ktb = ktc.astype(BF16)
        kdb = (ktc * kdec_ref[...]).astype(BF16)
        kv = []
        for h in range(R_HEADS):
            vh = vb[tok, h * R_DV:(h + 1) * R_DV]
            sc = _dot(qms[h][tok], ktb) * dmat_ref[h]
            o_in.append(_dot(sc.astype(BF16), vh))
            kv.append(_dot(kdb[h * R_DK:(h + 1) * R_DK], vh))
        states.append(states[-1] * cdec_ref[...] + jnp.concatenate(kv, axis=0))
    for c in range(nchunk):
        tok = slice(c * chunk, (c + 1) * chunk)
        sb = states[c].astype(BF16)
        for h in range(R_HEADS):
            hs = slice(h * R_DV, (h + 1) * R_DV)
            o_h = o_in[c * R_HEADS + h] + _dot(qms[h][tok], sb) * qdec_ref[:, hs]
            o_ref[tok, hs] = _ret_finish(o_h, rg[tok, hs])
    s_scr[...] = states[-1]
    s_ref[0] = states[-1]


def _ret_prompt(z, rkt, rtab, ctab, consts, s0, batch, seq):
    chunk = math.gcd(seq, R_CHUNK)
    rows = math.gcd(seq, 4 * chunk)
    nc = seq // rows
    dmat, qdec, kdec, cdec = consts
    dk = R_HEADS * R_DK
    const = lambda shape: pl.BlockSpec(shape, lambda b, c: (0,) * len(shape))
    return pl.pallas_call(
        _ret_prompt_kernel,
        grid=(batch, nc),
        in_specs=[
            pl.BlockSpec((rows, W_RETBLK), lambda b, c: (b * nc + c, Z_RET // W_RETBLK)),
            pl.BlockSpec((dk, rows), lambda b, c: (0, b * nc + c)),
            pl.BlockSpec((3, rows, LANE), lambda b, c: (0, c, 0)),
            pl.BlockSpec((2, R_DK // 2, rows), lambda b, c: (0, 0, c)),
            const((R_HEADS, chunk, chunk)), const((chunk, R_HEADS * R_DV)),
            const((dk, chunk)), const((dk, R_DV)),
            pl.BlockSpec((1, dk, R_DV), lambda b, c: (b, 0, 0)),
        ],
        out_specs=[pl.BlockSpec((rows, R_HEADS * R_DV), lambda b, c: (b * nc + c, 0)),
                   pl.BlockSpec((1, dk, R_DV), lambda b, c: (b, 0, 0))],
        out_shape=[jax.ShapeDtypeStruct((batch * seq, R_HEADS * R_DV), BF16),
                   jax.ShapeDtypeStruct((batch, dk, R_DV), F32)],
        scratch_shapes=[pltpu.VMEM((dk, R_DV), F32)],
        compiler_params=_params(("parallel", "arbitrary")),
        name="ret_prompt",
    )(z, rkt, rtab, ctab, dmat, qdec, kdec, cdec, s0)


def _ret_sample_kernel(zr_ref, rkt_ref, rtab_ref, ctab_ref, dmat_ref, qdec_ref, kdec_ref, cdec_ref,
                       s0_ref, o_ref, s_ref, ocr_scr, *, dec_seq):
    q, kt, v, rg = _ret_prepare(zr_ref, rkt_ref, rtab_ref, ctab_ref)
    n = q.shape[0]
    nseq = n // dec_seq
    ktb = kt.astype(BF16)
    kd = kt * kdec_ref[...]
    vb = v.astype(BF16)
    qms = [jnp.where(_head_lane_mask(h, n), q, 0.0).astype(BF16) for h in range(R_HEADS)]
    tok = lax.broadcasted_iota(jnp.int32, (R_DK, n), 1) // dec_seq

    for b in range(nseq):
        r0 = b * dec_seq
        sb = s0_ref[b].astype(BF16)
        qstack = jnp.concatenate([qm[r0:r0 + dec_seq] for qm in qms], axis=0)
        res = _dot(qstack, sb)
        for h in range(R_HEADS):
            ocr_scr[r0:r0 + dec_seq, h * R_DV:(h + 1) * R_DV] = res[h * dec_seq:(h + 1) * dec_seq]

    for h in range(R_HEADS):
        vh = vb[:, h * R_DV:(h + 1) * R_DV]
        hs = slice(h * R_DV, (h + 1) * R_DV)
        sc = _dot(qms[h], ktb) * dmat_ref[h]
        o_h = _dot(sc.astype(BF16), vh) + ocr_scr[:, hs] * qdec_ref[:, hs]
        o_ref[:, hs] = _ret_finish(o_h, rg[:, hs])
        rows = slice(h * R_DK, (h + 1) * R_DK)
        kdh = kd[rows]
        for b in range(nseq):
            kb = jnp.where(tok == b, kdh, 0.0).astype(BF16)
            s_ref[b, rows, :] = s0_ref[b, rows, :] * cdec_ref[rows, :] + _dot(kb, vh)


def _ret_sample(z, rkt, rtab, ctab, consts, s0, dec_seq):
    t = z.shape[0]
    n = min(LANE, t)
    nseq = n // dec_seq
    dmat, qdec, kdec, cdec = consts
    dk = R_HEADS * R_DK
    const = lambda shape: pl.BlockSpec(shape, lambda s: (0,) * len(shape))
    return pl.pallas_call(
        functools.partial(_ret_sample_kernel, dec_seq=dec_seq),
        grid=(t // n,),
        in_specs=[
            pl.BlockSpec((n, W_RETBLK), lambda s: (s, Z_RET // W_RETBLK)),
            pl.BlockSpec((dk, n), lambda s: (0, s)),
            pl.BlockSpec((3, n, LANE), lambda s: (0, s, 0)),
            pl.BlockSpec((2, R_DK // 2, n), lambda s: (0, 0, s)),
            const((R_HEADS, n, n)), const((n, R_HEADS * R_DV)),
            const((dk, n)), const((dk, R_DV)),
            pl.BlockSpec((nseq, dk, R_DV), lambda s: (s, 0, 0)),
        ],
        out_specs=[pl.BlockSpec((n, R_HEADS * R_DV), lambda s: (s, 0)),
                   pl.BlockSpec((nseq, dk, R_DV), lambda s: (s, 0, 0))],
        out_shape=[jax.ShapeDtypeStruct((t, R_HEADS * R_DV), BF16),
                   jax.ShapeDtypeStruct((t // dec_seq, dk, R_DV), F32)],
        scratch_shapes=[pltpu.VMEM((n, R_HEADS * R_DV), F32)],
        compiler_params=_params(("parallel",)),
        name="ret_sample",
    )(z, rkt, rtab, ctab, dmat, qdec, kdec, cdec, s0)


def _merge_tail(y_conv, gate_ref, omla_ref, oret_ref, x_ref, wco_ref, wmo_ref, wro_ref, wo_ref, y_ref):
    br_a = _dot(y_conv.astype(BF16), wco_ref[...])
    br_b = _dot(omla_ref[...], wmo_ref[...])
    br_c = _dot(oret_ref[...], wro_ref[...])
    mixed = (jax.nn.sigmoid(gate_ref[:, :D_MODEL]) * br_a
             + jax.nn.sigmoid(gate_ref[:, D_MODEL:2 * D_MODEL]) * br_b
             + jax.nn.sigmoid(gate_ref[:, 2 * D_MODEL:]) * br_c)
    y_ref[...] = x_ref[...] + _dot(mixed.astype(BF16), wo_ref[...])


def _merge_prompt_kernel(zc_ref, gate_ref, omla_ref, oret_ref, x_ref, st_ref, wc_ref,
                         wco_ref, wmo_ref, wro_ref, wo_ref, y_ref, nb_ref, ext_scr):
    tm = zc_ref.shape[0]

    @pl.when(pl.program_id(1) == 0)
    def _():
        ext_scr[0:SUBLANE, :] = st_ref[0]

    cb = zc_ref[:, :CONV_DIM]
    u = zc_ref[:, CONV_DIM:2 * CONV_DIM] * zc_ref[:, 2 * CONV_DIM:]
    ext_scr[SUBLANE:SUBLANE + tm, :] = u
    conv = (ext_scr[SUBLANE - 2:SUBLANE - 2 + tm, :] * wc_ref[0:1, :]
            + ext_scr[SUBLANE - 1:SUBLANE - 1 + tm, :] * wc_ref[1:2, :]
            + u * wc_ref[2:3, :])
    tail = ext_scr[tm:tm + SUBLANE, :]
    ext_scr[0:SUBLANE, :] = tail
    nb_ref[0] = tail
    _merge_tail(cb * conv, gate_ref, omla_ref, oret_ref, x_ref, wco_ref, wmo_ref, wro_ref, wo_ref, y_ref)


def _merge_sample_kernel(zc_ref, gate_ref, omla_ref, oret_ref, x_ref, st0_ref, st1_ref, wc_ref,
                         wco_ref, wmo_ref, wro_ref, wo_ref, y_ref, u_ref, *, dec_seq):
    cb = zc_ref[:, :CONV_DIM]
    u = zc_ref[:, CONV_DIM:2 * CONV_DIM] * zc_ref[:, 2 * CONV_DIM:]
    u_ref[...] = u
    pos = lax.broadcasted_iota(jnp.int32, u.shape, 0) % dec_seq
    um1 = jnp.where(pos == 0, st1_ref[...], pltpu.roll(u, 1, 0))
    um2 = jnp.where(pos == 0, st0_ref[...], jnp.where(pos == 1, st1_ref[...], pltpu.roll(u, 2, 0)))
    conv = um2 * wc_ref[0:1, :] + um1 * wc_ref[1:2, :] + u * wc_ref[2:3, :]
    _merge_tail(cb * conv, gate_ref, omla_ref, oret_ref, x_ref, wco_ref, wmo_ref, wro_ref, wo_ref, y_ref)


def _merge_weight_specs(ngrid):
    const = lambda shape: pl.BlockSpec(shape, lambda *_: (0,) * len(shape))
    return [const((SUBLANE, CONV_DIM)), const((CONV_DIM, D_MODEL)), const((HEADS * VDIM, D_MODEL)),
            const((R_HEADS * R_DV, D_MODEL)), const((D_MODEL, D_MODEL))]


def _merge_prompt(z, omla, oret, x, st_pad, weights, batch, seq):
    tm = min(512, seq)
    nt = seq // tm
    row = lambda b, i: b * nt + i
    return pl.pallas_call(
        _merge_prompt_kernel,
        grid=(batch, nt),
        in_specs=[
            pl.BlockSpec((tm, W_CONVBLK), lambda b, i: (row(b, i), Z_CONV // W_CONVBLK)),
            pl.BlockSpec((tm, 3 * D_MODEL), lambda b, i: (row(b, i), Z_GATE // (3 * D_MODEL))),
            pl.BlockSpec((tm, HEADS * VDIM), lambda b, i: (row(b, i), 0)),
            pl.BlockSpec((tm, R_HEADS * R_DV), lambda b, i: (row(b, i), 0)),
            pl.BlockSpec((tm, D_MODEL), lambda b, i: (row(b, i), 0)),
            pl.BlockSpec((1, SUBLANE, CONV_DIM), lambda b, i: (b, 0, 0)),
        ] + _merge_weight_specs(2),
        out_specs=[pl.BlockSpec((tm, D_MODEL), lambda b, i: (row(b, i), 0)),
                   pl.BlockSpec((1, SUBLANE, CONV_DIM), lambda b, i: (b, 0, 0))],
        out_shape=[jax.ShapeDtypeStruct((batch * seq, D_MODEL), F32),
                   jax.ShapeDtypeStruct((batch, SUBLANE, CONV_DIM), F32)],
        scratch_shapes=[pltpu.VMEM((tm + SUBLANE, CONV_DIM), F32)],
        compiler_params=_params(("parallel", "arbitrary")),
        name="merge_prompt",
    )(z, z, omla, oret, x, st_pad, *weights)


def _merge_sample(z, omla, oret, x, st0, st1, weights, dec_seq):
    t = z.shape[0]
    tm = min(512, t)
    return pl.pallas_call(
        functools.partial(_merge_sample_kernel, dec_seq=dec_seq),
        grid=(t // tm,),
        in_specs=[
            pl.BlockSpec((tm, W_CONVBLK), lambda i: (i, Z_CONV // W_CONVBLK)),
            pl.BlockSpec((tm, 3 * D_MODEL), lambda i: (i, Z_GATE // (3 * D_MODEL))),
            pl.BlockSpec((tm, HEADS * VDIM), lambda i: (i, 0)),
            pl.BlockSpec((tm, R_HEADS * R_DV), lambda i: (i, 0)),
            pl.BlockSpec((tm, D_MODEL), lambda i: (i, 0)),
            pl.BlockSpec((tm, CONV_DIM), lambda i: (i, 0)),
            pl.BlockSpec((tm, CONV_DIM), lambda i: (i, 0)),
        ] + _merge_weight_specs(1),
        out_specs=[pl.BlockSpec((tm, D_MODEL), lambda i: (i, 0)),
                   pl.BlockSpec((tm, CONV_DIM), lambda i: (i, 0))],
        out_shape=[jax.ShapeDtypeStruct((t, D_MODEL), F32),
                   jax.ShapeDtypeStruct((t, CONV_DIM), F32)],
        compiler_params=_params(("parallel",)),
        name="merge_sample",
    )(z, z, omla, oret, x, st0, st1, *weights)


def _ffn_kernel(x_ref, g_ref, wu_ref, wd_ref, y_ref, h_scr, acc_scr):
    j = pl.program_id(1)

    @pl.when(j == 0)
    def _():
        x = x_ref[...]
        ms = jnp.mean(x * x, axis=-1, keepdims=True)
        h_scr[...] = (x * lax.rsqrt(ms + EPS) * g_ref[...]).astype(BF16)
        acc_scr[...] = x

    a = jnp.maximum(_dot(h_scr[...], wu_ref[...]), 0.0)
    acc_scr[...] += _dot((a * a).astype(BF16), wd_ref[...])

    @pl.when(j == pl.num_programs(1) - 1)
    def _():
        y_ref[...] = acc_scr[...]


def _ffn(x, g, wu, wd):
    t = x.shape[0]
    tm = min(1024, t)
    tf = 1024
    return pl.pallas_call(
        _ffn_kernel,
        grid=(t // tm, D_FF // tf),
        in_specs=[
            pl.BlockSpec((tm, D_MODEL), lambda i, j: (i, 0)),
            pl.BlockSpec((1, D_MODEL), lambda i, j: (0, 0)),
            pl.BlockSpec((D_MODEL, tf), lambda i, j: (0, j)),
            pl.BlockSpec((tf, D_MODEL), lambda i, j: (j, 0)),
        ],
        out_specs=pl.BlockSpec((tm, D_MODEL), lambda i, j: (i, 0)),
        out_shape=jax.ShapeDtypeStruct((t, D_MODEL), F32),
        scratch_shapes=[pltpu.VMEM((tm, D_MODEL), BF16), pltpu.VMEM((tm, D_MODEL), F32)],
        compiler_params=_params(("parallel", "arbitrary")),
        name="ffn",
    )(x, g, wu, wd)


def _rope_tables(pos):
    posf = pos.astype(F32)[:, None]
    n = pos.shape[0]

    def cs(half):
        inv = THETA ** (-jnp.arange(half, dtype=F32) / half)
        ang = posf * inv[None, :]
        return jnp.cos(ang), jnp.sin(ang)

    c16, s16 = cs(ROPE // 2)
    one = lambda w: jnp.ones((n, w), F32)
    zero = lambda w: jnp.zeros((n, w), F32)
    qtab = jnp.stack([
        jnp.concatenate([one(NOPE), c16, c16, one(HT - QK)], axis=1),
        jnp.concatenate([zero(NOPE), -s16, zero(HT - NOPE - ROPE // 2)], axis=1),
        jnp.concatenate([zero(NOPE + ROPE // 2), s16, zero(HT - QK)], axis=1)])
    ktab = jnp.stack([
        jnp.concatenate([c16, c16, one(LANE - ROPE)], axis=1),
        jnp.concatenate([-s16, zero(LANE - ROPE // 2)], axis=1),
        jnp.concatenate([zero(ROPE // 2), s16, zero(LANE - ROPE)], axis=1)])
    c32, s32 = cs(R_DK // 2)
    z32 = zero(R_DK // 2)
    rtab = jnp.stack([
        jnp.concatenate([c32] * (LANE // (R_DK // 2)), axis=1),
        jnp.concatenate([-s32, z32] * (LANE // R_DK), axis=1),
        jnp.concatenate([z32, s32] * (LANE // R_DK), axis=1)])
    ctab = jnp.stack([c32.T, s32.T])
    return qtab, ktab, rtab, ctab


def _ret_consts(n, chunk):
    log_g = jnp.log1p(-(2.0 ** (-5.0 - jnp.arange(R_HEADS, dtype=F32))))
    idx = jnp.arange(n)
    loc = (idx % chunk).astype(F32)
    same = (idx[:, None] // chunk) == (idx[None, :] // chunk)
    diff = loc[:, None] - loc[None, :]
    dmat = jnp.where(same & (diff >= 0), jnp.exp(log_g[:, None, None] * jnp.maximum(diff, 0.0)), 0.0)
    qd = jnp.exp(log_g[None, :] * (loc[:, None] + 1.0))
    qdec = jnp.repeat(qd, R_DV, axis=1)
    kd = jnp.exp(log_g[:, None] * (chunk - 1.0 - loc[None, :]))
    kdec = jnp.repeat(kd, R_DK, axis=0)
    cdec = jnp.repeat(jnp.exp(log_g * chunk)[:, None], R_DK, axis=0) * jnp.ones((1, R_DV), F32)
    return dmat.astype(F32), qdec, kdec, cdec


def _layer_weights(l, g_mix, w_in, w_conv, w_conv_out, g_q_lat, w_uq, g_kv_lat, w_ukv, g_qn, g_kn,
                   w_mla_out, w_ret_out, w_o, g_ffn, w_up, w_down):
    wi = w_in[l]
    o_cq, o_rq, o_rk, o_gl, o_end = 1536, 2208, 2464, 3744, 6816
    w_perm = jnp.concatenate(
        [wi[:, :o_cq], wi[:, o_rq:o_gl], wi[:, o_gl:o_end], wi[:, o_cq:o_rq],
         jnp.zeros((D_MODEL, LANE - ROPE), F32)], axis=1).astype(BF16)
    w_rk_t = wi[:, o_rk:o_rk + R_HEADS * R_DK].T.astype(BF16)
    wuq = jnp.pad(w_uq[l].reshape(Q_LORA, HEADS, QK), ((0, 0), (0, 0), (0, HT - QK)))
    wuq = wuq.reshape(Q_LORA, HEADS * HT).astype(BF16)
    wkv = w_ukv[l].reshape(KV_LORA, HEADS, NOPE + VDIM)
    w_uk, w_uv = wkv[..., :NOPE], wkv[..., NOPE:]
    wk_pad = jnp.pad(w_uk, ((0, 0), (0, 0), (0, HT - NOPE))).reshape(KV_LORA, HEADS * HT).astype(BF16)
    wv = w_uv.reshape(KV_LORA, HEADS * VDIM).astype(BF16)
    wv_pad = jnp.pad(w_uv, ((0, 0), (0, 0), (0, HT - VDIM))).reshape(KV_LORA, HEADS * HT).T.astype(BF16)
    wukt = w_uk.reshape(KV_LORA, HEADS * NOPE).T.astype(BF16)
    j = np.arange(ROPE)
    pm = np.zeros((LANE, HEADS * HT), np.float32)
    for h in range(HEADS):
        pm[j, h * HT + NOPE + j] = 1.0
    shift = np.zeros((HT, LANE), np.float32)
    shift[NOPE + j, j] = 1.0
    wabs = jnp.pad(jnp.transpose(w_uk, (1, 2, 0)), ((0, 0), (0, HT - NOPE), (0, 0)))
    wcomb = jnp.concatenate([wabs, jnp.broadcast_to(jnp.asarray(shift), (HEADS, HT, LANE))],
                            axis=2).astype(BF16)
    pad_gain = lambda g: jnp.pad(g, (0, HT - QK))[None, :]
    return dict(
        g_mix=g_mix[l][None, :], w_perm=w_perm, w_rk_t=w_rk_t,
        g_q_lat=g_q_lat[l][None, :], wuq=wuq, g_kv_lat=g_kv_lat[l][None, :],
        gqn=pad_gain(g_qn[l]) * (QK ** -0.5 * math.log2(math.e)), gkn=pad_gain(g_kn[l]),
        wk_pad=wk_pad, pmat=jnp.asarray(pm, BF16), wv=wv, wv_pad=wv_pad, wukt=wukt, wcomb=wcomb,
        merge=(jnp.pad(w_conv[l], ((0, SUBLANE - CONV_W), (0, 0))), w_conv_out[l].astype(BF16),
               w_mla_out[l].astype(BF16), w_ret_out[l].astype(BF16), w_o[l].astype(BF16)),
        g_ffn=g_ffn[l][None, :], w_up=w_up[l].astype(BF16), w_down=w_down[l].astype(BF16))


def kernel(x_prompt, x_sample, cache_ckv, cache_kpe, state_conv, state_ret, page_table, g_mix, w_in,
           w_conv, w_conv_out, g_q_lat, w_uq, g_kv_lat, w_ukv, g_qn, g_kn, w_mla_out, w_ret_out, w_o,
           g_ffn, w_up, w_down):
    batch, seq, _ = x_prompt.shape
    nb, dec_seq, _ = x_sample.shape
    depth = w_in.shape[0]
    past = page_table.shape[1] * PAGE
    assert dec_seq == SUBLANE
    tp, ts = batch * seq, nb * dec_seq
    dk = R_HEADS * R_DK

    tabs_p = _rope_tables(jnp.arange(seq))
    tabs_s = _rope_tables(jnp.tile(past + jnp.arange(dec_seq), nb))
    chunk_p = math.gcd(seq, R_CHUNK)
    consts_p = _ret_consts(chunk_p, chunk_p)
    consts_s = _ret_consts(min(LANE, ts), dec_seq)
    tm_p, tm_s = min(512, seq), min(512, ts)

    cache_kpe_t = jnp.swapaxes(cache_kpe, 2, 3)
    yp = x_prompt.reshape(tp, D_MODEL)
    ys = x_sample.reshape(ts, D_MODEL)
    conv0 = jnp.zeros((batch, SUBLANE, CONV_DIM), F32)
    ret0 = jnp.zeros((batch, dk, R_DV), F32)
    outs = {k: [] for k in ("ckv_p", "kpe_p", "conv_p", "ret_p", "ckv_s", "kpe_s", "conv_s", "ret_s")}

    for l in range(depth):
        w = _layer_weights(l, g_mix, w_in, w_conv, w_conv_out, g_q_lat, w_uq, g_kv_lat, w_ukv, g_qn,
                           g_kn, w_mla_out, w_ret_out, w_o, g_ffn, w_up, w_down)

        qtab, ktab, rtab, ctab = tabs_p
        z, rkt = _inproj(yp, w["g_mix"], w["w_perm"], w["w_rk_t"])
        q, ckvn, kper, k, v = _mlaprep(z, qtab, ktab, tm_p, w["g_q_lat"], w["wuq"], w["g_kv_lat"],
                                       w["gqn"], (w["wk_pad"], w["pmat"], w["wv_pad"], w["gkn"]))
        omla = _flash(q, k, v, batch, seq)
        oret, snew = _ret_prompt(z, rkt, rtab, ctab, consts_p, ret0, batch, seq)
        ymid, nbuf = _merge_prompt(z, omla, oret, yp, conv0, w["merge"], batch, seq)
        yp = _ffn(ymid, w["g_ffn"], w["w_up"], w["w_down"])
        outs["ckv_p"].append(ckvn.reshape(batch, seq, KV_LORA))
        outs["kpe_p"].append(kper.reshape(batch, seq, ROPE))
        outs["conv_p"].append(nbuf[:, SUBLANE - (CONV_W - 1):, :])
        outs["ret_p"].append(snew.reshape(batch, R_HEADS, R_DK, R_DV))

        qtab, ktab, rtab, ctab = tabs_s
        z, rkt = _inproj(ys, w["g_mix"], w["w_perm"], w["w_rk_t"])
        q, ckvn, kper = _mlaprep(z, qtab, ktab, tm_s, w["g_q_lat"], w["wuq"], w["g_kv_lat"],
                                 w["gqn"], None)
        qa, qp = _qabs(q, w["gkn"], w["wcomb"], nb, dec_seq)
        cnew = jnp.pad(ckvn.reshape(nb, dec_seq, KV_LORA), ((0, 0), (0, PAGE - dec_seq), (0, 0)))
        pnew_t = jnp.pad(jnp.swapaxes(kper.reshape(nb, dec_seq, ROPE), 1, 2),
                         ((0, 0), (0, 0), (0, PAGE - dec_seq)))
        omla = _decode(page_table, l, cache_ckv, cache_kpe_t,
                       qa.reshape(nb, HEADS * dec_seq, KV_LORA), qp.reshape(nb, HEADS * dec_seq, LANE),
                       w["wukt"], w["wv"], cnew, pnew_t).reshape(ts, HEADS * VDIM)
        oret, snew = _ret_sample(z, rkt, rtab, ctab, consts_s, state_ret[l].reshape(nb, dk, R_DV), dec_seq)
        st = state_conv[l]
        st0 = jnp.repeat(st[:, 0, :], dec_seq, axis=0)
        st1 = jnp.repeat(st[:, 1, :], dec_seq, axis=0)
        ymid, u = _merge_sample(z, omla, oret, ys, st0, st1, w["merge"], dec_seq)
        ys = _ffn(ymid, w["g_ffn"], w["w_up"], w["w_down"])
        outs["ckv_s"].append(ckvn.reshape(nb, dec_seq, KV_LORA))
        outs["kpe_s"].append(kper.reshape(nb, dec_seq, ROPE))
        outs["conv_s"].append(u.reshape(nb, dec_seq, CONV_DIM)[:, dec_seq - (CONV_W - 1):, :])
        outs["ret_s"].append(snew.reshape(nb, R_HEADS, R_DK, R_DV))

    return (yp.reshape(batch, seq, D_MODEL), ys.reshape(nb, dec_seq, D_MODEL),
            jnp.stack(outs["ckv_p"]), jnp.stack(outs["kpe_p"]), jnp.stack(outs["conv_p"]),
            jnp.stack(outs["ret_p"]),
            jnp.stack(outs["ckv_s"]), jnp.stack(outs["kpe_s"]), jnp.stack(outs["conv_s"]),
            jnp.stack(outs["ret_s"]))
```
